```python
import jax, jax.numpy as jnp
from jax import lax
import numpy as np

D_MODEL = 2048
BATCH = 8
SEQ = 4096
DEPTH = 1
DEC_BATCH = 8
DEC_SEQ = 16
PAST_LEN = 4096

CHUNK = 64
H_A = 8
DK = 128
DV = 128
D_A = H_A * DV
G_B = 8
DG_B = 128
D_B = G_B * DG_B
CHUNK_B = 128
N_EXPERTS = 32
TOP_K = 4
D_FF = D_MODEL
SWIGLU_LIMIT = 7.0
SWIGLU_ALPHA = 1.702
P_DIM = 256
MOE_BLOCK = 256
EPS = 1e-6
SPLITS = (H_A * DK, 2 * H_A * DK, 2 * H_A * DK + D_A, 2 * H_A * DK + 2 * D_A,
          2 * H_A * DK + 2 * D_A + D_B, 2 * H_A * DK + 2 * D_A + 2 * D_B,
          2 * H_A * DK + 2 * D_A + 2 * D_B + D_MODEL)
N_IN = 2 * H_A * DK + 2 * D_A + 2 * D_B + 2 * D_MODEL

kernel_name = 'hgrn2_gmlp_moe_streaming_step'


def rmsnorm(x, g):
    xf = x.astype(jnp.float32)
    y = xf * lax.rsqrt(jnp.mean(xf * xf, axis=-1, keepdims=True) + EPS)
    return (y * g.astype(jnp.float32)).astype(x.dtype)


def layernorm(x, g):
    xf = x.astype(jnp.float32)
    xc = xf - jnp.mean(xf, axis=-1, keepdims=True)
    y = xc * lax.rsqrt(jnp.mean(xc * xc, axis=-1, keepdims=True) + EPS)
    return (y * g.astype(jnp.float32)).astype(x.dtype)


def hgrn2_chunk(s0, q, k, v, logf):
    c = q.shape[1]
    b = jnp.cumsum(logf, axis=1)
    causal = jnp.tril(jnp.ones((c, c), dtype=bool))[None, :, :, None, None]
    decay = jnp.exp(jnp.where(causal, b[:, :, None] - b[:, None, :], -jnp.inf))
    scores = jnp.einsum('bthk,btshk,bshk->bhts', q, decay, k)
    o = (jnp.einsum('bhts,bshv->bthv', scores, v)
         + jnp.einsum('bthk,bhkv->bthv', q * jnp.exp(b), s0))
    b_last = b[:, -1]
    s_new = (jnp.exp(b_last)[..., None] * s0
             + jnp.einsum('bshk,bshv->bhkv', k * jnp.exp(b_last[:, None] - b), v))
    return s_new, o


def hgrn2_recurrence(s0, q, k, v, logf):
    bsz, t = q.shape[:2]
    if t <= CHUNK:
        return hgrn2_chunk(s0, q, k, v, logf)
    n_chunks = t // CHUNK

    def to_chunks(a):
        return a.reshape(bsz, n_chunks, CHUNK, *a.shape[2:]).swapaxes(0, 1)

    s_final, o = lax.scan(lambda s, blk: hgrn2_chunk(s, *blk), s0,
                          (to_chunks(q), to_chunks(k), to_chunks(v), to_chunks(logf)))
    return s_final, o.swapaxes(0, 1).reshape(bsz, t, H_A, DV)


def spatial_gate(v, w_s, b_s):
    bsz, t = v.shape[:2]
    length = min(t, CHUNK_B)
    n_chunks = t // length
    w = jnp.where(jnp.tril(jnp.ones((length, length), dtype=bool)), w_s[:, :length, :length], 0.0)
    vg = v.reshape(bsz, n_chunks, length, G_B, DG_B)
    mixed = (jnp.einsum('gts,bcsgd->bctgd', w.astype(v.dtype), vg)
             + b_s[:, :length].T[None, None, :, :, None].astype(v.dtype))
    return mixed.reshape(bsz, t, D_B)


def clamped_swiglu(h):
    glu, lin = h[..., :D_FF], h[..., D_FF:]
    glu = jnp.minimum(glu, SWIGLU_LIMIT)
    lin = jnp.clip(lin, -SWIGLU_LIMIT, SWIGLU_LIMIT)
    return glu * jax.nn.sigmoid(SWIGLU_ALPHA * glu) * (lin + 1.0)


def moe_ffn(x2d, router_w, router_b, w_gate_up, b_gate_up, w_down, b_down):
    n = x2d.shape[0]
    logits = (x2d @ router_w).astype(jnp.float32) + router_b.astype(jnp.float32)
    top_val, top_idx = lax.top_k(logits, TOP_K)
    gates = jax.nn.softmax(top_val, axis=-1)
    n_pairs = n * TOP_K
    flat_e = top_idx.reshape(-1)
    order = jnp.argsort(flat_e)
    sorted_e = flat_e[order]
    pair_tok = (order // TOP_K).astype(jnp.int32)
    pair_w = gates.reshape(-1)[order]
    counts = jnp.bincount(flat_e, length=N_EXPERTS)
    padded = (counts + MOE_BLOCK - 1) // MOE_BLOCK * MOE_BLOCK
    padded_end = jnp.cumsum(padded)
    rank = jnp.arange(n_pairs) - (jnp.cumsum(counts) - counts)[sorted_e]
    dest = (padded_end - padded)[sorted_e] + rank
    n_blocks = (n_pairs + N_EXPERTS * (MOE_BLOCK - 1) + MOE_BLOCK - 1) // MOE_BLOCK
    rows = n_blocks * MOE_BLOCK
    row_tok = jnp.full((rows,), n, jnp.int32).at[dest].set(pair_tok)
    row_w = jnp.zeros((rows,), jnp.float32).at[dest].set(pair_w)
    block_e = jnp.minimum(jnp.searchsorted(padded_end, jnp.arange(n_blocks) * MOE_BLOCK, side='right'),
                          N_EXPERTS - 1)
    x_pad = jnp.concatenate([x2d, jnp.zeros((1, x2d.shape[1]), x2d.dtype)], axis=0)

    def body(acc, blk):
        tok, wt, e = blk
        hgu = x_pad[tok] @ w_gate_up[e] + b_gate_up[e]
        y = clamped_swiglu(hgu) @ w_down[e] + b_down[e]
        return acc.at[tok].add(y.astype(jnp.float32) * wt[:, None]), None

    acc, _ = lax.scan(body, jnp.zeros((n + 1, x2d.shape[1]), jnp.float32),
                      (row_tok.reshape(n_blocks, MOE_BLOCK), row_w.reshape(n_blocks, MOE_BLOCK), block_e))
    return acc[:n].astype(x2d.dtype)


def trunk_layer(x, pe, s0, lb, g_mix_pre, g_mix_post, g_ffn_pre, g_ffn_post, g_ple_pre, g_ple_post,
                w_in, g_hgrn_out, g_gmlp_v, w_spatial, b_spatial, w_branch_a, w_branch_b, w_out,
                router_w, router_b, w_gate_up, b_gate_up, w_down, b_down, w_ple_proj, w_ple_gate):
    bsz, t, _ = x.shape
    f32 = jnp.float32
    xn = rmsnorm(x, g_mix_pre)
    proj = xn @ w_in
    q, f, i, og, u, v, gate_a, gate_b = jnp.split(proj, SPLITS, axis=-1)
    fg = lb + (1.0 - lb) * jax.nn.sigmoid(f.astype(f32))
    q4 = jax.nn.silu(q.astype(f32)).reshape(bsz, t, H_A, DK)
    k4 = (1.0 - fg).reshape(bsz, t, H_A, DK)
    logf = jnp.log(fg).reshape(bsz, t, H_A, DK)
    i4 = i.astype(f32).reshape(bsz, t, H_A, DV)
    s_new, o = hgrn2_recurrence(s0.astype(f32), q4, k4, i4, logf)
    o = rmsnorm(o, g_hgrn_out) * jax.nn.sigmoid(og.astype(f32)).reshape(bsz, t, H_A, DV)
    y_a = o.reshape(bsz, t, D_A).astype(x.dtype)
    zu = jax.nn.gelu(u)
    vn = layernorm(jax.nn.gelu(v), g_gmlp_v)
    y_b = zu * spatial_gate(vn, w_spatial, b_spatial)
    mixed = jax.nn.sigmoid(gate_a) * (y_a @ w_branch_a) + jax.nn.sigmoid(gate_b) * (y_b @ w_branch_b)
    h = x + rmsnorm(mixed @ w_out, g_mix_post)
    ffn = moe_ffn(rmsnorm(h, g_ffn_pre).reshape(bsz * t, D_MODEL), router_w, router_b,
                  w_gate_up, b_gate_up, w_down, b_down).reshape(bsz, t, D_MODEL)
    h = h + rmsnorm(ffn, g_ffn_post)
    gate = jax.nn.sigmoid(rmsnorm(h, g_ple_pre) @ w_ple_gate)
    h = h + rmsnorm(gate * (pe @ w_ple_proj), g_ple_post)
    return h, s_new, vn


def setup_inputs(seed: int = 0) -> dict:
    key = jax.random.key(seed)
    ks = jax.random.split(key, 28)
    f32 = jnp.float32

    def nrm(k, shape, scale):
        return jax.random.normal(k, shape, f32) * scale

    def gain(k, shape):
        return 1.0 + 0.05 * jax.random.normal(k, shape, f32)

    return {
        'x_prompt': nrm(ks[0], (BATCH, SEQ, D_MODEL), 1.0),
        'x_sample': nrm(ks[1], (DEC_BATCH, DEC_SEQ, D_MODEL), 1.0),
        'state_hgrn': nrm(ks[2], (DEPTH, DEC_BATCH, H_A, DK, DV), 0.5),
        'p_prompt': nrm(ks[3], (DEPTH, BATCH, SEQ, P_DIM), 1.0),
        'p_sample': nrm(ks[4], (DEPTH, DEC_BATCH, DEC_SEQ, P_DIM), 1.0),
        'norm_mix_pre': gain(ks[5], (DEPTH, D_MODEL)),
        'norm_mix_post': gain(ks[6], (DEPTH, D_MODEL)),
        'norm_ffn_pre': gain(ks[7], (DEPTH, D_MODEL)),
        'norm_ffn_post': gain(ks[8], (DEPTH, D_MODEL)),
        'norm_ple_pre': gain(ks[9], (DEPTH, D_MODEL)),
        'norm_ple_post': gain(ks[10], (DEPTH, D_MODEL)),
        'w_in': nrm(ks[11], (DEPTH, D_MODEL, N_IN), D_MODEL ** -0.5),
        'lb_logits': nrm(ks[12], (DEPTH + 1, H_A * DK), 0.5),
        'hgrn_out_norm': gain(ks[13], (DEPTH, DV)),
        'gmlp_v_norm': gain(ks[14], (DEPTH, D_B)),
        'w_spatial': nrm(ks[15], (DEPTH, G_B, CHUNK_B, CHUNK_B), CHUNK_B ** -0.5),
        'b_spatial': gain(ks[16], (DEPTH, G_B, CHUNK_B)),
        'w_branch_a': nrm(ks[17], (DEPTH, D_A, D_MODEL), D_A ** -0.5),
        'w_branch_b': nrm(ks[18], (DEPTH, D_B, D_MODEL), D_B ** -0.5),
        'w_out': nrm(ks[19], (DEPTH, D_MODEL, D_MODEL), D_MODEL ** -0.5),
        'router_w': nrm(ks[20], (DEPTH, D_MODEL, N_EXPERTS), D_MODEL ** -0.5),
        'router_b': nrm(ks[21], (DEPTH, N_EXPERTS), 0.01),
        'w_gate_up': nrm(ks[22], (DEPTH, N_EXPERTS, D_MODEL, 2 * D_FF), D_MODEL ** -0.5),
        'b_gate_up': nrm(ks[23], (DEPTH, N_EXPERTS, 2 * D_FF), 0.01),
        'w_down': nrm(ks[24], (DEPTH, N_EXPERTS, D_FF, D_MODEL), D_FF ** -0.5),
        'b_down': nrm(ks[25], (DEPTH, N_EXPERTS, D_MODEL), 0.01),
        'w_ple_proj': nrm(ks[26], (DEPTH, P_DIM, D_MODEL), P_DIM ** -0.5),
        'w_ple_gate': nrm(ks[27], (DEPTH, D_MODEL, D_MODEL), D_MODEL ** -0.5),
    }


def reference(x_prompt, x_sample, state_hgrn, p_prompt, p_sample, norm_mix_pre, norm_mix_post,
              norm_ffn_pre, norm_ffn_post, norm_ple_pre, norm_ple_post, w_in, lb_logits,
              hgrn_out_norm, gmlp_v_norm, w_spatial, b_spatial, w_branch_a, w_branch_b, w_out,
              router_w, router_b, w_gate_up, b_gate_up, w_down, b_down, w_ple_proj, w_ple_gate):
    lb_all = jnp.cumsum(jax.nn.softmax(lb_logits.astype(jnp.float32), axis=0), axis=0)
    y_prompt, y_sample = x_prompt, x_sample
    s_prompt_rows, s_sample_rows, v_sample_rows = [], [], []
    for l in range(DEPTH):
        layer_w = (norm_mix_pre[l], norm_mix_post[l], norm_ffn_pre[l], norm_ffn_post[l],
                   norm_ple_pre[l], norm_ple_post[l], w_in[l], hgrn_out_norm[l], gmlp_v_norm[l],
                   w_spatial[l], b_spatial[l], w_branch_a[l], w_branch_b[l], w_out[l],
                   router_w[l], router_b[l], w_gate_up[l], b_gate_up[l], w_down[l], b_down[l],
                   w_ple_proj[l], w_ple_gate[l])
        s0_prompt = jnp.zeros((x_prompt.shape[0], H_A, DK, DV), jnp.float32)
        y_prompt, s_p, _ = trunk_layer(y_prompt, p_prompt[l], s0_prompt, lb_all[l], *layer_w)
        y_sample, s_s, v_s = trunk_layer(y_sample, p_sample[l], state_hgrn[l], lb_all[l], *layer_w)
        s_prompt_rows.append(s_p)
        s_sample_rows.append(s_s)
        v_sample_rows.append(v_s)
    state_hgrn_prompt = jnp.stack(s_prompt_rows, axis=0)
    state_hgrn_sample = jnp.stack(s_sample_rows, axis=0)
    gmlp_v_sample = jnp.stack(v_sample_rows, axis=0)
    return (y_prompt, y_sample, state_hgrn_prompt, state_hgrn_sample, gmlp_v_sample)
```

```python
import functools

import jax
import jax.numpy as jnp
from jax import lax
from jax.experimental import pallas as pl
from jax.experimental.pallas import tpu as pltpu

F32 = jnp.float32
BF16 = jnp.bfloat16
I32 = jnp.int32

EPS = 1e-6
H_A = 8
DK = 128
DV = 128
D_A = H_A * DV
G_B = 8
DG_B = 128
D_B = G_B * DG_B
CHUNK_B = 128
N_EXPERTS = 32
TOP_K = 4
SWIGLU_LIMIT = 7.0
SWIGLU_ALPHA = 1.702

LANES = 128
SEG = 1024
MOE_BM = 512
MOE_TF = 512
VMEM_LIMIT = 56 * 1024 * 1024


def _mm(a, b):
    return jnp.dot(a.astype(BF16), b.astype(BF16), preferred_element_type=F32)


def _mm_nt(a, b):
    return lax.dot_general(a.astype(BF16), b.astype(BF16), (((1,), (1,)), ((), ())),
                           preferred_element_type=F32)


def _mm_tn(a, b):
    return lax.dot_general(a.astype(BF16), b.astype(BF16), (((0,), (0,)), ((), ())),
                           preferred_element_type=F32)


def _cumsum_rows(x):
    n = x.shape[0]
    r_i = lax.broadcasted_iota(I32, (n, n), 0)
    c_i = lax.broadcasted_iota(I32, (n, n), 1)
    return jnp.dot((r_i >= c_i).astype(F32), x, preferred_element_type=F32,
                   precision=lax.Precision.HIGHEST)


def _rms(x, g):
    return x * lax.rsqrt(jnp.mean(x * x, axis=-1, keepdims=True) + EPS) * g


def _resident(shape):
    zeros = (0,) * len(shape)
    return pl.BlockSpec(shape, lambda *_: zeros, pipeline_mode=pl.Buffered(1))


def _proj_col(j):
    return j - (j >= 1).astype(I32) - (j >= 4).astype(I32)


def _proj_kernel(x_ref, g_ref, w_ref, gv_ref, wsp_ref, bsp_ref, p_ref, f_ref, *rest,
                 chunk, emit_vn):
    if emit_vn:
        vn_ref, xn_s, zu_s, vn_s = rest
    else:
        xn_s, zu_s, vn_s = rest
    j = pl.program_id(1)
    tm = x_ref.shape[0]
    cdt = p_ref.dtype

    @pl.when(j == 0)
    def _():
        xn_s[...] = _rms(x_ref[...], g_ref[...]).astype(xn_s.dtype)

    acc = _mm(xn_s[...], w_ref[...])

    @pl.when(j == 0)
    def _():
        p_ref[...] = (acc * jax.nn.sigmoid(acc)).astype(cdt)

    @pl.when(j == 1)
    def _():
        f_ref[...] = acc

    @pl.when(j == 2)
    def _():
        p_ref[...] = acc.astype(cdt)

    @pl.when(j == 3)
    def _():
        p_ref[...] = jax.nn.sigmoid(acc).astype(cdt)

    @pl.when(j == 4)
    def _():
        zu_s[...] = jax.nn.gelu(acc).astype(zu_s.dtype)

    @pl.when(j == 5)
    def _():
        gv = jax.nn.gelu(acc)
        xc = gv - jnp.mean(gv, axis=-1, keepdims=True)
        vn = xc * lax.rsqrt(jnp.mean(xc * xc, axis=-1, keepdims=True) + EPS) * gv_ref[...]
        vn_s[...] = vn
        if emit_vn:
            vn_ref[...] = vn
        for c in range(tm // chunk):
            rows = slice(c * chunk, (c + 1) * chunk)
            for g in range(G_B):
                cols = slice(g * DG_B, (g + 1) * DG_B)
                m = _mm(wsp_ref[g], vn_s[rows, cols]) + bsp_ref[g]
                p_ref[rows, cols] = (zu_s[rows, cols].astype(F32) * m).astype(cdt)

    @pl.when(j >= 6)
    def _():
        p_ref[...] = jax.nn.sigmoid(acc).astype(cdt)


def _proj(x2d, g_pre, w_in, g_v, wsp, bsp, *, cdt, chunk, tm, emit_vn):
    rows, d = x2d.shape
    n_in = w_in.shape[1]
    nj = n_in // SEG
    out_shape = [jax.ShapeDtypeStruct((rows, (nj - 2) * SEG), cdt),
                 jax.ShapeDtypeStruct((rows, SEG), F32)]
    out_specs = [pl.BlockSpec((tm, SEG), lambda i, j: (i, _proj_col(j))),
                 pl.BlockSpec((tm, SEG), lambda i, j: (i, 0))]
    if emit_vn:
        out_shape.append(jax.ShapeDtypeStruct((rows, SEG), F32))
        out_specs.append(pl.BlockSpec((tm, SEG), lambda i, j: (i, 0)))
    return pl.pallas_call(
        functools.partial(_proj_kernel, chunk=chunk, emit_vn=emit_vn),
        grid=(rows // tm, nj),
        in_specs=[pl.BlockSpec((tm, d), lambda i, j: (i, 0)),
                  pl.BlockSpec((1, d), lambda i, j: (0, 0)),
                  pl.BlockSpec((d, SEG), lambda i, j: (0, j)),
                  pl.BlockSpec((1, SEG), lambda i, j: (0, 0)),
                  pl.BlockSpec((G_B, chunk, chunk), lambda i, j: (0, 0, 0)),
                  pl.BlockSpec((G_B, chunk, DG_B), lambda i, j: (0, 0, 0))],
        out_specs=out_specs,
        out_shape=out_shape,
        scratch_shapes=[pltpu.VMEM((tm, d), BF16), pltpu.VMEM((tm, SEG), cdt),
                        pltpu.VMEM((tm, SEG), F32)],
        compiler_params=pltpu.CompilerParams(
            dimension_semantics=("arbitrary", "arbitrary"), vmem_limit_bytes=VMEM_LIMIT),
        name="proj",
    )(x2d, g_pre, w_in, g_v, wsp, bsp)


def _scores_split(q, k, b):
    c_len = q.shape[0]
    b_mid = b[c_len // 2 - 1:c_len // 2, :]
    sc = _mm_nt(q * jnp.exp(b - b_mid), k * jnp.exp(b_mid - b))
    r_i = lax.broadcasted_iota(I32, (c_len, c_len), 0)
    c_i = lax.broadcasted_iota(I32, (c_len, c_len), 1)
    return jnp.where(r_i >= c_i, sc, 0.0)


def _scores_pairwise(q, k, b):
    c_len = q.shape[0]
    qb = q.astype(BF16).astype(F32)
    t_i = lax.broadcasted_iota(I32, b.shape, 0)
    lane = lax.broadcasted_iota(I32, (c_len, c_len), 1)
    sc = jnp.zeros((c_len, c_len), F32)
    for s in range(c_len):
        diff = jnp.where(t_i >= s, b - b[s:s + 1, :], -jnp.inf)
        dk = (jnp.exp(diff) * k[s:s + 1, :]).astype(BF16).astype(F32)
        sc = jnp.where(lane == s, jnp.sum(qb * dk, axis=-1, keepdims=True), sc)
    return sc


def _hgrn_kernel(q_ref, i_ref, og_ref, f_ref, lb_ref, g_ref, s0_ref, ya_ref, sout_ref, st_s,
                 *, c_len, pairwise):
    ti = pl.program_id(1)
    nt = pl.num_programs(1)

    @pl.when(ti == 0)
    def _():
        for h in range(H_A):
            st_s[h] = s0_ref[h].T

    tt = q_ref.shape[0]

    def chunk(c, carry):
        rows = pl.ds(pl.multiple_of(c * c_len, c_len), c_len)
        for h in range(H_A):
            hl = slice(h * DK, (h + 1) * DK)
            lb = lb_ref[:, hl]
            fg = lb + (1.0 - lb) * jax.nn.sigmoid(f_ref[rows, hl])
            k = 1.0 - fg
            b = _cumsum_rows(jnp.log(fg))
            b_last = b[c_len - 1:c_len, :]
            q = q_ref[rows, hl].astype(F32)
            v = i_ref[rows, hl]
            sc = _scores_pairwise(q, k, b) if pairwise else _scores_split(q, k, b)
            st = st_s[h]
            o = _mm(sc, v) + _mm_nt(q * jnp.exp(b), st)
            st_s[h] = jnp.exp(b_last) * st + _mm_tn(v, k * jnp.exp(b_last - b))
            on = _rms(o, g_ref[...])
            ya_ref[rows, hl] = (on * og_ref[rows, hl].astype(F32)).astype(ya_ref.dtype)
        return carry

    lax.fori_loop(0, tt // c_len, chunk, 0)

    @pl.when(ti == nt - 1)
    def _():
        for h in range(H_A):
            sout_ref[h] = st_s[h].T


def _hgrn(p, f, lb, g_out, s0, *, bsz, t, c_len, tt, pairwise):
    nt = t // tt
    cdt = p.dtype
    row = lambda b, ti: b * nt + ti
    return pl.pallas_call(
        functools.partial(_hgrn_kernel, c_len=c_len, pairwise=pairwise),
        grid=(bsz, nt),
        in_specs=[pl.BlockSpec((tt, D_A), lambda b, ti: (row(b, ti), 0)),
                  pl.BlockSpec((tt, D_A), lambda b, ti: (row(b, ti), 1)),
                  pl.BlockSpec((tt, D_A), lambda b, ti: (row(b, ti), 2)),
                  pl.BlockSpec((tt, D_A), lambda b, ti: (row(b, ti), 0)),
                  pl.BlockSpec((1, D_A), lambda b, ti: (0, 0)),
                  pl.BlockSpec((1, DV), lambda b, ti: (0, 0)),
                  pl.BlockSpec((None, H_A, DK, DV), lambda b, ti: (b, 0, 0, 0))],
        out_specs=[pl.BlockSpec((tt, D_A), lambda b, ti: (row(b, ti), 0)),
                   pl.BlockSpec((None, H_A, DK, DV), lambda b, ti: (b, 0, 0, 0))],
        out_shape=[jax.ShapeDtypeStruct((bsz * t, D_A), cdt),
                   jax.ShapeDtypeStruct((bsz, H_A, DK, DV), F32)],
        scratch_shapes=[pltpu.VMEM((H_A, DV, DK), F32)],
        compiler_params=pltpu.CompilerParams(
            dimension_semantics=("arbitrary", "arbitrary"), vmem_limit_bytes=VMEM_LIMIT),
        name="hgrn",
    )(p, p, p, f, lb, g_out, s0)


def _mix_kernel(ya_ref, yb_ref, sga_ref, sgb_ref, x_ref, wa_ref, wb_ref, wo_ref, gpost_ref,
                gpre_ref, rw_ref, rb_ref, c0_ref, h_ref, hn_ref, meta_ref, gate_ref, cnt_ref,
                carry_s):
    i = pl.program_id(0)
    tm = x_ref.shape[0]

    @pl.when(i == 0)
    def _():
        carry_s[...] = c0_ref[...]

    ma = _mm(ya_ref[...], wa_ref[...])
    mb = _mm(yb_ref[...], wb_ref[...])
    mixed = sga_ref[...].astype(F32) * ma + sgb_ref[...].astype(F32) * mb
    h = x_ref[...] + _rms(_mm(mixed, wo_ref[...]), gpost_ref[...])
    h_ref[...] = h
    hn = _rms(h, gpre_ref[...])
    hn_ref[...] = hn
    logits = _mm(hn, rw_ref[...]) + rb_ref[...]

    e_i = lax.broadcasted_iota(I32, (tm, N_EXPERTS), 1)
    work = logits
    vals, idxs = [], []
    onehot = jnp.zeros((tm, N_EXPERTS), F32)
    for _ in range(TOP_K):
        mx = jnp.max(work, axis=-1, keepdims=True)
        idx = jnp.min(jnp.where(work == mx, e_i, N_EXPERTS), axis=-1, keepdims=True)
        sel = e_i == idx
        vals.append(mx)
        idxs.append(idx)
        onehot = onehot + sel.astype(F32)
        work = jnp.where(sel, -jnp.inf, work)
    exps = [jnp.exp(v - vals[0]) for v in vals]
    denom = exps[0] + exps[1] + exps[2] + exps[3]

    r_i = lax.broadcasted_iota(I32, (tm, tm), 0)
    c_i = lax.broadcasted_iota(I32, (tm, tm), 1)
    before = (r_i > c_i).astype(BF16)
    prefix = jnp.dot(before, onehot.astype(BF16), preferred_element_type=F32) + carry_s[...]
    carry_s[...] = carry_s[...] + jnp.sum(onehot, axis=0, keepdims=True)
    cnt_ref[...] = carry_s[...]

    lane = lax.broadcasted_iota(I32, (tm, LANES), 1)
    meta = jnp.zeros((tm, LANES), I32)
    gts = jnp.zeros((tm, LANES), F32)
    for k in range(TOP_K):
        rank = jnp.sum(jnp.where(e_i == idxs[k], prefix, 0.0), axis=-1, keepdims=True)
        meta = jnp.where(lane == k, idxs[k], meta)
        meta = jnp.where(lane == TOP_K + k, rank.astype(I32), meta)
        gts = jnp.where(lane == k, exps[k] / denom, gts)
    meta_ref[...] = meta
    gate_ref[...] = gts


def _mix(ya, p, x2d, wa, wb, wo, g_post, g_pre, rw, rb, c0, *, tm):
    rows, d = x2d.shape
    wide = d // SEG
    return pl.pallas_call(
        _mix_kernel,
        grid=(rows // tm,),
        in_specs=[pl.BlockSpec((tm, D_A), lambda i: (i, 0)),
                  pl.BlockSpec((tm, D_B), lambda i: (i, 3)),
                  pl.BlockSpec((tm, d), lambda i: (i, 4 // wide)),
                  pl.BlockSpec((tm, d), lambda i: (i, 4 // wide + 1)),
                  pl.BlockSpec((tm, d), lambda i: (i, 0)),
                  _resident(wa.shape), _resident(wb.shape), _resident(wo.shape),
                  _resident((1, d)), _resident((1, d)),
                  _resident(rw.shape), _resident((1, N_EXPERTS)), _resident((1, N_EXPERTS))],
        out_specs=[pl.BlockSpec((tm, d), lambda i: (i, 0)),
                   pl.BlockSpec((tm, d), lambda i: (i, 0)),
                   pl.BlockSpec((tm, LANES), lambda i: (i, 0)),
                   pl.BlockSpec((tm, LANES), lambda i: (i, 0)),
                   pl.BlockSpec((1, N_EXPERTS), lambda i: (0, 0))],
        out_shape=[jax.ShapeDtypeStruct((rows, d), F32),
                   jax.ShapeDtypeStruct((rows, d), F32),
                   jax.ShapeDtypeStruct((rows, LANES), I32),
                   jax.ShapeDtypeStruct((rows, LANES), F32),
                   jax.ShapeDtypeStruct((1, N_EXPERTS), F32)],
        scratch_shapes=[pltpu.VMEM((1, N_EXPERTS), F32)],
        compiler_params=pltpu.CompilerParams(
            dimension_semantics=("arbitrary",), vmem_limit_bytes=VMEM_LIMIT),
        name="mix",
    )(ya, p, p, p, x2d, wa, wb, wo, g_post, g_pre, rw, rb, c0)


def _row_copy(src_ref, src_row, dst_ref, dst_row, sem):
    return pltpu.make_async_copy(src_ref.at[pl.ds(src_row, 1), :],
                                 dst_ref.at[pl.ds(dst_row, 1), :], sem)


def _dispatch_kernel(dest_ref, hn_ref, xs_in_ref, xs_ref, sem):
    del xs_in_ref
    tm = hn_ref.shape[0]

    def issue(t, carry):
        for k in range(TOP_K):
            _row_copy(hn_ref, t, xs_ref, dest_ref[0, 0, t * TOP_K + k], sem).start()
        return carry

    lax.fori_loop(0, tm, issue, 0)

    def drain(t, carry):
        for k in range(TOP_K):
            _row_copy(hn_ref, t, xs_ref, dest_ref[0, 0, t * TOP_K + k], sem).wait()
        return carry

    lax.fori_loop(0, tm, drain, 0)


def _dispatch(dest3, hn, xs, *, tm):
    rows, d = hn.shape
    return pl.pallas_call(
        _dispatch_kernel,
        grid=(rows // tm,),
        in_specs=[pl.BlockSpec((1, 1, tm * TOP_K), lambda i: (i, 0, 0), memory_space=pltpu.SMEM),
                  pl.BlockSpec((tm, d), lambda i: (i, 0)),
                  pl.BlockSpec(memory_space=pl.ANY)],
        out_specs=pl.BlockSpec(memory_space=pl.ANY),
        out_shape=jax.ShapeDtypeStruct(xs.shape, xs.dtype),
        scratch_shapes=[pltpu.SemaphoreType.DMA(())],
        input_output_aliases={2: 0},
        compiler_params=pltpu.CompilerParams(
            dimension_semantics=("arbitrary",), has_side_effects=True),
        name="dispatch",
    )(dest3, hn, xs)


def _experts_kernel(be_ref, nu_ref, x_ref, wg_ref, wl_ref, bg_ref, bl_ref, wd_ref, bd_ref,
                    y_ref, xb_s):
    del be_ref
    b = pl.program_id(0)
    j = pl.program_id(1)

    @pl.when(b < nu_ref[0])
    def _():
        @pl.when(j == 0)
        def _():
            xb_s[...] = x_ref[...].astype(BF16)

        xb = xb_s[...]
        glu = jnp.dot(xb, wg_ref[...], preferred_element_type=F32) + bg_ref[...]
        lin = jnp.dot(xb, wl_ref[...], preferred_element_type=F32) + bl_ref[...]
        glu = jnp.minimum(glu, SWIGLU_LIMIT)
        lin = jnp.clip(lin, -SWIGLU_LIMIT, SWIGLU_LIMIT)
        act = glu * jax.nn.sigmoid(SWIGLU_ALPHA * glu) * (lin + 1.0)
        part = jnp.dot(act.astype(BF16), wd_ref[...], preferred_element_type=F32)

        @pl.when(j == 0)
        def _():
            y_ref[...] = part + bd_ref[...]

        @pl.when(j > 0)
        def _():
            y_ref[...] = y_ref[...] + part

    @pl.when(jnp.logical_and(b >= nu_ref[0], j == 0))
    def _():
        y_ref[...] = jnp.zeros_like(y_ref)


def _experts(block_e, n_used, xs, w_gate_up, b_gate_up, w_down, b_down):
    rows, d = xs.shape
    d_ff = w_down.shape[1]
    nb = rows // MOE_BM
    nj = d_ff // MOE_TF

    def blk(b, nu):
        return jnp.minimum(b, nu[0] - 1)

    def jj(b, j, nu):
        return jnp.where(b < nu[0], j, nj - 1)

    grid_spec = pltpu.PrefetchScalarGridSpec(
        num_scalar_prefetch=2,
        grid=(nb, nj),
        in_specs=[
            pl.BlockSpec((MOE_BM, d), lambda b, j, be, nu: (blk(b, nu), 0)),
            pl.BlockSpec((None, d, MOE_TF), lambda b, j, be, nu: (be[blk(b, nu)], 0, jj(b, j, nu))),
            pl.BlockSpec((None, d, MOE_TF),
                         lambda b, j, be, nu: (be[blk(b, nu)], 0, nj + jj(b, j, nu))),
            pl.BlockSpec((None, 1, MOE_TF), lambda b, j, be, nu: (be[blk(b, nu)], 0, jj(b, j, nu))),
            pl.BlockSpec((None, 1, MOE_TF),
                         lambda b, j, be, nu: (be[blk(b, nu)], 0, nj + jj(b, j, nu))),
            pl.BlockSpec((None, MOE_TF, d), lambda b, j, be, nu: (be[blk(b, nu)], jj(b, j, nu), 0)),
            pl.BlockSpec((None, 1, d), lambda b, j, be, nu: (be[blk(b, nu)], 0, 0)),
        ],
        out_specs=pl.BlockSpec((MOE_BM, d), lambda b, j, be, nu: (b, 0)),
        scratch_shapes=[pltpu.VMEM((MOE_BM, d), BF16)],
    )
    return pl.pallas_call(
        _experts_kernel,
        grid_spec=grid_spec,
        out_shape=jax.ShapeDtypeStruct((rows, d), F32),
        compiler_params=pltpu.CompilerParams(
            dimension_semantics=("arbitrary", "arbitrary"), vmem_limit_bytes=VMEM_LIMIT),
        name="experts",
    )(block_e, n_used, xs, w_gate_up, w_gate_up, b_gate_up, b_gate_up, w_down, b_down)


def _combine_kernel(dest_ref, gate_ref, h_ref, pe_ref, gpost_ref, gpre_ref, gple_ref, wgate_ref,
                    wproj_ref, ys_ref, out_ref, buf, sem):
    tm = h_ref.shape[0]

    def issue(t, carry):
        for k in range(TOP_K):
            _row_copy(ys_ref, dest_ref[0, 0, t * TOP_K + k], buf.at[k], t, sem).start()
        return carry

    lax.fori_loop(0, tm, issue, 0)

    def drain(t, carry):
        for k in range(TOP_K):
            _row_copy(ys_ref, dest_ref[0, 0, t * TOP_K + k], buf.at[k], t, sem).wait()
        return carry

    lax.fori_loop(0, tm, drain, 0)

    gts = gate_ref[...]
    ffn = gts[:, 0:1] * buf[0]
    for k in range(1, TOP_K):
        ffn = ffn + gts[:, k:k + 1] * buf[k]
    h = h_ref[...] + _rms(ffn, gpost_ref[...])
    gate = jax.nn.sigmoid(_mm(_rms(h, gpre_ref[...]), wgate_ref[...]))
    pp = _mm(pe_ref[...], wproj_ref[...])
    out_ref[...] = h + _rms(gate * pp, gple_ref[...])


def _combine(dest3, gates, h, pe, g_post, g_pre, g_ple, w_gate, w_proj, ys, *, tm):
    rows, d = h.shape
    pdim = pe.shape[1]
    return pl.pallas_call(
        _combine_kernel,
        grid=(rows // tm,),
        in_specs=[pl.BlockSpec((1, 1, tm * TOP_K), lambda i: (i, 0, 0), memory_space=pltpu.SMEM),
                  pl.BlockSpec((tm, LANES), lambda i: (i, 0)),
                  pl.BlockSpec((tm, d), lambda i: (i, 0)),
                  pl.BlockSpec((tm, pdim), lambda i: (i, 0)),
                  _resident((1, d)), _resident((1, d)), _resident((1, d)),
                  _resident(w_gate.shape), _resident(w_proj.shape),
                  pl.BlockSpec(memory_space=pl.ANY)],
        out_specs=pl.BlockSpec((tm, d), lambda i: (i, 0)),
        out_shape=jax.ShapeDtypeStruct((rows, d), F32),
        scratch_shapes=[pltpu.VMEM((TOP_K, tm, d), F32), pltpu.SemaphoreType.DMA(())],
        compiler_params=pltpu.CompilerParams(
            dimension_semantics=("arbitrary",), vmem_limit_bytes=VMEM_LIMIT),
        name="combine",
    )(dest3, gates, h, pe, g_post, g_pre, g_ple, w_gate, w_proj, ys)


def _spatial_params(w_spatial, b_spatial, length, dtype):
    tril = jnp.tril(jnp.ones((length, length), dtype=bool))
    w = jnp.where(tril, w_spatial[:, :length, :length], 0.0).astype(dtype)
    b = jnp.broadcast_to(b_spatial[:, :length, None], (G_B, length, DG_B)).astype(F32)
    return w, b


def _token_stage(x, s0, lb, w, *, cdt, tm_proj, tm_mix, c_len, tt, pairwise, c0, emit_vn):
    bsz, t, d = x.shape
    x2d = x.reshape(bsz * t, d)
    length = min(t, CHUNK_B)
    wsp, bsp = _spatial_params(w["w_spatial"], w["b_spatial"], length, BF16)
    outs = _proj(x2d, w["g_mix_pre"], w["w_in"], w["g_gmlp_v"], wsp, bsp,
                 cdt=cdt, chunk=length, tm=tm_proj, emit_vn=emit_vn)
    p, f = outs[0], outs[1]
    vn = outs[2] if emit_vn else None
    ya, s_new = _hgrn(p, f, lb, w["g_hgrn_out"], s0, bsz=bsz, t=t, c_len=c_len, tt=tt,
                      pairwise=pairwise)
    h, hn, meta, gates, cnt = _mix(
        ya, p, x2d, w["w_branch_a"], w["w_branch_b"], w["w_out"], w["g_mix_post"],
        w["g_ffn_pre"], w["router_w"], w["router_b"], c0, tm=tm_mix)
    return h, hn, meta, gates, cnt, s_new, vn


def _layer(xp, xs, pe_p, pe_s, s0_s, lb, w):
    bp, tp, d = xp.shape
    bs, ts, _ = xs.shape
    n_p, n_s = bp * tp, bs * ts

    zero_cnt = jnp.zeros((1, N_EXPERTS), F32)
    s0_p = jnp.zeros((bp, H_A, DK, DV), F32)
    h_p, hn_p, meta_p, gates_p, cnt_p, st_p, _ = _token_stage(
        xp, s0_p, lb, w, cdt=BF16, tm_proj=512, tm_mix=256, c_len=32, tt=256, pairwise=False,
        c0=zero_cnt, emit_vn=False)
    h_s, hn_s, meta_s, gates_s, cnt, st_s, vn_s = _token_stage(
        xs, s0_s, lb, w, cdt=F32, tm_proj=n_s, tm_mix=n_s, c_len=ts, tt=ts, pairwise=True,
        c0=cnt_p, emit_vn=True)

    counts = cnt[0].astype(I32)
    padded = (counts + MOE_BM - 1) // MOE_BM * MOE_BM
    pend = jnp.cumsum(padded)
    offs = pend - padded
    n_pairs = (n_p + n_s) * TOP_K
    nb = (n_pairs + N_EXPERTS * (MOE_BM - 1) + MOE_BM - 1) // MOE_BM
    block_e = jnp.minimum(jnp.searchsorted(pend, jnp.arange(nb, dtype=I32) * MOE_BM, side="right"),
                          N_EXPERTS - 1).astype(I32)
    n_used = (pend[-1:] // MOE_BM).astype(I32)

    def dest_of(meta):
        eidx, rank = meta[:, :TOP_K], meta[:, TOP_K:2 * TOP_K]
        hit = eidx[:, :, None] == jnp.arange(N_EXPERTS, dtype=I32)[None, None, :]
        return jnp.sum(jnp.where(hit, offs[None, None, :], 0), axis=-1) + rank

    tm_d = 128
    dest_p = dest_of(meta_p).reshape(n_p // tm_d, 1, tm_d * TOP_K)
    dest_s = dest_of(meta_s).reshape(1, 1, n_s * TOP_K)

    xs_rows = jnp.zeros((nb * MOE_BM, d), F32)
    xs_rows = _dispatch(dest_p, hn_p, xs_rows, tm=tm_d)
    xs_rows = _dispatch(dest_s, hn_s, xs_rows, tm=n_s)
    ys_rows = _experts(block_e, n_used, xs_rows, w["w_gate_up"], w["b_gate_up"][:, None, :],
                       w["w_down"], w["b_down"][:, None, :])

    wg, wp = w["w_ple_gate"], w["w_ple_proj"]
    y_p = _combine(dest_p, gates_p, h_p, pe_p.reshape(n_p, -1), w["g_ffn_post"], w["g_ple_pre"],
                   w["g_ple_post"], wg, wp, ys_rows, tm=tm_d)
    y_s = _combine(dest_s, gates_s, h_s, pe_s.reshape(n_s, -1), w["g_ffn_post"], w["g_ple_pre"],
                   w["g_ple_post"], wg, wp, ys_rows, tm=n_s)
    return (y_p.reshape(bp, tp, d), y_s.reshape(bs, ts, d), st_p, st_s,
            vn_s.reshape(bs, ts, D_B))


def kernel(x_prompt, x_sample, state_hgrn, p_prompt, p_sample, norm_mix_pre, norm_mix_post,
           norm_ffn_pre, norm_ffn_post, norm_ple_pre, norm_ple_post, w_in, lb_logits,
           hgrn_out_norm, gmlp_v_norm, w_spatial, b_spatial, w_branch_a, w_branch_b, w_out,
           router_w, router_b, w_gate_up, b_gate_up, w_down, b_down, w_ple_proj, w_ple_gate):
    depth = w_in.shape[0]
    lb_all = jnp.cumsum(jax.nn.softmax(lb_logits.astype(F32), axis=0), axis=0)
    y_p, y_s = x_prompt, x_sample
    st_p_rows, st_s_rows, vn_rows = [], [], []
    for l in range(depth):
        row = lambda a: a[l][None, :]
        mat = lambda a: a[l].astype(BF16)
        w = dict(g_mix_pre=row(norm_mix_pre), g_mix_post=row(norm_mix_post),
                 g_ffn_pre=row(norm_ffn_pre), g_ffn_post=row(norm_ffn_post),
                 g_ple_pre=row(norm_ple_pre), g_ple_post=row(norm_ple_post),
                 w_in=mat(w_in), g_hgrn_out=row(hgrn_out_norm), g_gmlp_v=row(gmlp_v_norm),
                 w_spatial=w_spatial[l], b_spatial=b_spatial[l], w_branch_a=mat(w_branch_a),
                 w_branch_b=mat(w_branch_b), w_out=mat(w_out), router_w=mat(router_w),
                 router_b=row(router_b), w_gate_up=mat(w_gate_up), b_gate_up=b_gate_up[l],
                 w_down=mat(w_down), b_down=b_down[l], w_ple_proj=mat(w_ple_proj),
                 w_ple_gate=mat(w_ple_gate))
        y_p, y_s, st_p, st_s, vn_s = _layer(y_p, y_s, p_prompt[l], p_sample[l],
                                            state_hgrn[l].astype(F32), row(lb_all), w)
        st_p_rows.append(st_p)
        st_s_rows.append(st_s)
        vn_rows.append(vn_s)
    return (y_p, y_s, jnp.stack(st_p_rows, axis=0), jnp.stack(st_s_rows, axis=0),
            jnp.stack(vn_rows, axis=0))
```

```python
import functools

import jax
import jax.numpy as jnp
from jax import lax
from jax.experimental import pallas as pl
from jax.experimental.pallas import tpu as pltpu

F32 = jnp.float32
BF16 = jnp.bfloat16
I32 = jnp.int32

EPS = 1e-6
H_A = 8
DK = 128
DV = 128
D_A = H_A * DV
G_B = 8
DG_B = 128
D_B = G_B * DG_B
CHUNK_B = 128
N_EXPERTS = 32
TOP_K = 4
SWIGLU_LIMIT = 7.0
SWIGLU_ALPHA = 1.702

LANES = 128
SEG = 1024
MOE_BM = 512
MOE_TF = 512
VMEM_LIMIT = 56 * 1024 * 1024


def _mm(a, b):
    return jnp.dot(a.astype(BF16), b.astype(BF16), preferred_element_type=F32)


def _mm_nt(a, b):
    return lax.dot_general(a.astype(BF16), b.astype(BF16), (((1,), (1,)), ((), ())),
                           preferred_element_type=F32)


def _mm_tn(a, b):
    return lax.dot_general(a.astype(BF16), b.astype(BF16), (((0,), (0,)), ((), ())),
                           preferred_element_type=F32)


def _rms(x, g):
    return x * lax.rsqrt(jnp.mean(x * x, axis=-1, keepdims=True) + EPS) * g


def _resident(shape):
    zeros = (0,) * len(shape)
    return pl.BlockSpec(shape, lambda *_: zeros, pipeline_mode=pl.Buffered(1))


def _proj_col(j):
    return j - (j >= 1).astype(I32) - (j >= 4).astype(I32)


def _proj_kernel(x_ref, g_ref, w_ref, gv_ref, wsp_ref, bsp_ref, p_ref, f_ref, *rest,
                 chunk, emit_vn):
    if emit_vn:
        vn_ref, xn_s, zu_s, vn_s = rest
    else:
        xn_s, zu_s, vn_s = rest
    j = pl.program_id(1)
    tm = x_ref.shape[0]
    cdt = p_ref.dtype

    @pl.when(j == 0)
    def _():
        xn_s[...] = _rms(x_ref[...], g_ref[...]).astype(xn_s.dtype)

    def acc():
        return _mm(xn_s[...], w_ref[...])

    @pl.when(j == 0)
    def _():
        a = acc()
        p_ref[...] = (a * jax.nn.sigmoid(a)).astype(cdt)

    @pl.when(j == 1)
    def _():
        f_ref[...] = acc()

    @pl.when(j == 2)
    def _():
        p_ref[...] = acc().astype(cdt)

    @pl.when(j == 3)
    def _():
        p_ref[...] = jax.nn.sigmoid(acc()).astype(cdt)

    @pl.when(j == 4)
    def _():
        zu_s[...] = jax.nn.gelu(acc()).astype(zu_s.dtype)

    @pl.when(j == 5)
    def _():
        gv = jax.nn.gelu(acc())
        xc = gv - jnp.mean(gv, axis=-1, keepdims=True)
        vn = xc * lax.rsqrt(jnp.mean(xc * xc, axis=-1, keepdims=True) + EPS) * gv_ref[...]
        vn_s[...] = vn
        if emit_vn:
            vn_ref[...] = vn
        for c in range(tm // chunk):
            rows = slice(c * chunk, (c + 1) * chunk)
            for g in range(G_B):
                cols = slice(g * DG_B, (g + 1) * DG_B)
                m = _mm(wsp_ref[g], vn_s[rows, cols]) + bsp_ref[g]
                p_ref[rows, cols] = (zu_s[rows, cols].astype(F32) * m).astype(cdt)

    @pl.when(j >= 6)
    def _():
        p_ref[...] = jax.nn.sigmoid(acc()).astype(cdt)


def _proj(x2d, g_pre, w_in, g_v, wsp, bsp, *, cdt, chunk, tm, emit_vn):
    rows, d = x2d.shape
    n_in = w_in.shape[1]
    nj = n_in // SEG
    out_shape = [jax.ShapeDtypeStruct((rows, (nj - 2) * SEG), cdt),
                 jax.ShapeDtypeStruct((rows, SEG), F32)]
    out_specs = [pl.BlockSpec((tm, SEG), lambda i, j: (i, _proj_col(j))),
                 pl.BlockSpec((tm, SEG), lambda i, j: (i, 0))]
    if emit_vn:
        out_shape.append(jax.ShapeDtypeStruct((rows, SEG), F32))
        out_specs.append(pl.BlockSpec((tm, SEG), lambda i, j: (i, 0)))
    return pl.pallas_call(
        functools.partial(_proj_kernel, chunk=chunk, emit_vn=emit_vn),
        grid=(rows // tm, nj),
        in_specs=[pl.BlockSpec((tm, d), lambda i, j: (i, 0)),
                  pl.BlockSpec((1, d), lambda i, j: (0, 0)),
                  pl.BlockSpec((d, SEG), lambda i, j: (0, j)),
                  pl.BlockSpec((1, SEG), lambda i, j: (0, 0)),
                  pl.BlockSpec((G_B, chunk, chunk), lambda i, j: (0, 0, 0)),
                  pl.BlockSpec((G_B, chunk, DG_B), lambda i, j: (0, 0, 0))],
        out_specs=out_specs,
        out_shape=out_shape,
        scratch_shapes=[pltpu.VMEM((tm, d), BF16), pltpu.VMEM((tm, SEG), cdt),
                        pltpu.VMEM((tm, SEG), F32)],
        compiler_params=pltpu.CompilerParams(
            dimension_semantics=("arbitrary", "arbitrary"), vmem_limit_bytes=VMEM_LIMIT),
        name="proj",
    )(x2d, g_pre, w_in, g_v, wsp, bsp)


def _scores_pairwise(q, k, b):
    c_len = q.shape[0]
    qb = q.astype(BF16).astype(F32)
    t_i = lax.broadcasted_iota(I32, b.shape, 0)
    lane = lax.broadcasted_iota(I32, (c_len, c_len), 1)
    sc = jnp.zeros((c_len, c_len), F32)
    for s in range(c_len):
        diff = jnp.where(t_i >= s, b - b[s:s + 1, :], -jnp.inf)
        dk = (jnp.exp(diff) * k[s:s + 1, :]).astype(BF16).astype(F32)
        sc = jnp.where(lane == s, jnp.sum(qb * dk, axis=-1, keepdims=True), sc)
    return sc


def _hgrn_kernel(q_ref, i_ref, og_ref, f_ref, lb_ref, g_ref, tri_ref, s0_ref, ya_ref, sout_ref,
                 st_s, qs_s, ks_s, qb_s, kl_s, dl_s, o_s, *, c_len, pairwise):
    ti = pl.program_id(1)
    nt = pl.num_programs(1)

    @pl.when(ti == 0)
    def _():
        for h in range(H_A):
            st_s[h] = s0_ref[h].T

    tt = q_ref.shape[0]
    nc = tt // c_len
    mid = c_len // 2

    lb = lb_ref[...]
    fg = lb + (1.0 - lb) * jax.nn.sigmoid(f_ref[...])
    k = 1.0 - fg
    logf = jnp.log(fg)
    hi = logf.astype(BF16)
    rem = logf - hi.astype(F32)
    md = rem.astype(BF16)
    lo = (rem - md.astype(F32)).astype(BF16)
    tri = tri_ref[...]
    b = (jnp.dot(tri, hi, preferred_element_type=F32) + jnp.dot(tri, md, preferred_element_type=F32)
         + jnp.dot(tri, lo, preferred_element_type=F32))
    q = q_ref[...].astype(F32)
    b3 = b.reshape(nc, c_len, D_A)
    k3 = k.reshape(nc, c_len, D_A)
    b_last = b3[:, c_len - 1:c_len, :]
    qb_s[...] = (q * jnp.exp(b)).astype(BF16)
    kl_s[...] = (k3 * jnp.exp(b_last - b3)).reshape(tt, D_A).astype(BF16)
    dl_s[...] = jnp.exp(b_last)
    if not pairwise:
        b_mid = b3[:, mid - 1:mid, :]
        qs_s[...] = (q.reshape(nc, c_len, D_A) * jnp.exp(b3 - b_mid)).reshape(tt, D_A).astype(BF16)
        ks_s[...] = (k3 * jnp.exp(b_mid - b3)).reshape(tt, D_A).astype(BF16)
    r_i = lax.broadcasted_iota(I32, (c_len, c_len), 0)
    c_i = lax.broadcasted_iota(I32, (c_len, c_len), 1)
    causal = r_i >= c_i

    def chunk(c, carry):
        rows = slice(0, c_len) if nc == 1 else pl.ds(pl.multiple_of(c * c_len, c_len), c_len)
        dl = dl_s[c]
        for h in range(H_A):
            hl = slice(h * DK, (h + 1) * DK)
            if pairwise:
                sc = _scores_pairwise(q[:, hl], k[:, hl], b[:, hl])
            else:
                sc = jnp.where(causal, _mm_nt(qs_s[rows, hl], ks_s[rows, hl]), 0.0)
            st = st_s[h]
            v = i_ref[rows, hl]
            o_s[rows, hl] = _mm(sc, v) + _mm_nt(qb_s[rows, hl], st)
            st_s[h] = dl[:, hl] * st + _mm_tn(v, kl_s[rows, hl])
        return carry

    if nc == 1:
        chunk(0, 0)
    else:
        lax.fori_loop(0, nc, chunk, 0)

    for h in range(H_A):
        hl = slice(h * DK, (h + 1) * DK)
        on = _rms(o_s[:, hl], g_ref[...])
        ya_ref[:, hl] = (on * og_ref[:, hl].astype(F32)).astype(ya_ref.dtype)

    @pl.when(ti == nt - 1)
    def _():
        for h in range(H_A):
            sout_ref[h] = st_s[h].T


def _hgrn(p, f, lb, g_out, s0, *, bsz, t, c_len, tt, pairwise):
    assert t % tt == 0 and tt % c_len == 0 and (not pairwise or tt == c_len)
    nt = t // tt
    nc = tt // c_len
    cdt = p.dtype
    r_i = lax.broadcasted_iota(I32, (tt, tt), 0)
    c_i = lax.broadcasted_iota(I32, (tt, tt), 1)
    tri = jnp.logical_and(r_i >= c_i, r_i // c_len == c_i // c_len).astype(BF16)
    row = lambda b, ti: b * nt + ti
    return pl.pallas_call(
        functools.partial(_hgrn_kernel, c_len=c_len, pairwise=pairwise),
        grid=(bsz, nt),
        in_specs=[pl.BlockSpec((tt, D_A), lambda b, ti: (row(b, ti), 0)),
                  pl.BlockSpec((tt, D_A), lambda b, ti: (row(b, ti), 1)),
                  pl.BlockSpec((tt, D_A), lambda b, ti: (row(b, ti), 2)),
                  pl.BlockSpec((tt, D_A), lambda b, ti: (row(b, ti), 0)),
                  pl.BlockSpec((1, D_A), lambda b, ti: (0, 0)),
                  pl.BlockSpec((1, DV), lambda b, ti: (0, 0)),
                  pl.BlockSpec((tt, tt), lambda b, ti: (0, 0)),
                  pl.BlockSpec((None, H_A, DK, DV), lambda b, ti: (b, 0, 0, 0))],
        out_specs=[pl.BlockSpec((tt, D_A), lambda b, ti: (row(b, ti), 0)),
                   pl.BlockSpec((None, H_A, DK, DV), lambda b, ti: (b, 0, 0, 0))],
        out_shape=[jax.ShapeDtypeStruct((bsz * t, D_A), cdt),
                   jax.ShapeDtypeStruct((bsz, H_A, DK, DV), F32)],
        scratch_shapes=[pltpu.VMEM((H_A, DV, DK), F32),
                        pltpu.VMEM((tt, D_A), BF16), pltpu.VMEM((tt, D_A), BF16),
                        pltpu.VMEM((tt, D_A), BF16), pltpu.VMEM((tt, D_A), BF16),
                        pltpu.VMEM((nc, 1, D_A), F32), pltpu.VMEM((tt, D_A), F32)],
        compiler_params=pltpu.CompilerParams(
            dimension_semantics=("arbitrary", "arbitrary"), vmem_limit_bytes=VMEM_LIMIT),
        name="hgrn",
    )(p, p, p, f, lb, g_out, tri, s0)


def _mix_kernel(ya_ref, yb_ref, sga_ref, sgb_ref, x_ref, wa_ref, wb_ref, wo_ref, gpost_ref,
                gpre_ref, rw_ref, rb_ref, c0_ref, h_ref, hn_ref, meta_ref, gate_ref, cnt_ref,
                carry_s):
    i = pl.program_id(0)
    tm = x_ref.shape[0]

    @pl.when(i == 0)
    def _():
        carry_s[...] = c0_ref[...]

    ma = _mm(ya_ref[...], wa_ref[...])
    mb = _mm(yb_ref[...], wb_ref[...])
    mixed = sga_ref[...].astype(F32) * ma + sgb_ref[...].astype(F32) * mb
    h = x_ref[...] + _rms(_mm(mixed, wo_ref[...]), gpost_ref[...])
    h_ref[...] = h
    hn = _rms(h, gpre_ref[...])
    hn_ref[...] = hn
    logits = _mm(hn, rw_ref[...]) + rb_ref[...]

    e_i = lax.broadcasted_iota(I32, (tm, N_EXPERTS), 1)
    work = logits
    vals, idxs = [], []
    onehot = jnp.zeros((tm, N_EXPERTS), F32)
    for _ in range(TOP_K):
        mx = jnp.max(work, axis=-1, keepdims=True)
        idx = jnp.min(jnp.where(work == mx, e_i, N_EXPERTS), axis=-1, keepdims=True)
        sel = e_i == idx
        vals.append(mx)
        idxs.append(idx)
        onehot = onehot + sel.astype(F32)
        work = jnp.where(sel, -jnp.inf, work)
    exps = [jnp.exp(v - vals[0]) for v in vals]
    denom = exps[0] + exps[1] + exps[2] + exps[3]

    r_i = lax.broadcasted_iota(I32, (tm, tm), 0)
    c_i = lax.broadcasted_iota(I32, (tm, tm), 1)
    before = (r_i > c_i).astype(BF16)
    prefix = jnp.dot(before, onehot.astype(BF16), preferred_element_type=F32) + carry_s[...]
    carry_s[...] = carry_s[...] + jnp.sum(onehot, axis=0, keepdims=True)
    cnt_ref[...] = carry_s[...]

    lane = lax.broadcasted_iota(I32, (tm, LANES), 1)
    meta = jnp.zeros((tm, LANES), I32)
    gts = jnp.zeros((tm, LANES), F32)
    for k in range(TOP_K):
        rank = jnp.sum(jnp.where(e_i == idxs[k], prefix, 0.0), axis=-1, keepdims=True)
        meta = jnp.where(lane == k, idxs[k], meta)
        meta = jnp.where(lane == TOP_K + k, rank.astype(I32), meta)
        gts = jnp.where(lane == k, exps[k] / denom, gts)
    meta_ref[...] = meta
    gate_ref[...] = gts


def _mix(ya, p, x2d, wa, wb, wo, g_post, g_pre, rw, rb, c0, *, tm):
    rows, d = x2d.shape
    wide = d // SEG
    return pl.pallas_call(
        _mix_kernel,
        grid=(rows // tm,),
        in_specs=[pl.BlockSpec((tm, D_A), lambda i: (i, 0)),
                  pl.BlockSpec((tm, D_B), lambda i: (i, 3)),
                  pl.BlockSpec((tm, d), lambda i: (i, 4 // wide)),
                  pl.BlockSpec((tm, d), lambda i: (i, 4 // wide + 1)),
                  pl.BlockSpec((tm, d), lambda i: (i, 0)),
                  _resident(wa.shape), _resident(wb.shape), _resident(wo.shape),
                  _resident((1, d)), _resident((1, d)),
                  _resident(rw.shape), _resident((1, N_EXPERTS)), _resident((1, N_EXPERTS))],
        out_specs=[pl.BlockSpec((tm, d), lambda i: (i, 0)),
                   pl.BlockSpec((tm, d), lambda i: (i, 0)),
                   pl.BlockSpec((tm, LANES), lambda i: (i, 0)),
                   pl.BlockSpec((tm, LANES), lambda i: (i, 0)),
                   pl.BlockSpec((1, N_EXPERTS), lambda i: (0, 0))],
        out_shape=[jax.ShapeDtypeStruct((rows, d), F32),
                   jax.ShapeDtypeStruct((rows, d), F32),
                   jax.ShapeDtypeStruct((rows, LANES), I32),
                   jax.ShapeDtypeStruct((rows, LANES), F32),
                   jax.ShapeDtypeStruct((1, N_EXPERTS), F32)],
        scratch_shapes=[pltpu.VMEM((1, N_EXPERTS), F32)],
        compiler_params=pltpu.CompilerParams(
            dimension_semantics=("arbitrary",), vmem_limit_bytes=VMEM_LIMIT),
        name="mix",
    )(ya, p, p, p, x2d, wa, wb, wo, g_post, g_pre, rw, rb, c0)


def _row_copy(src_ref, src_row, dst_ref, dst_row, sem):
    return pltpu.make_async_copy(src_ref.at[pl.ds(src_row, 1), :],
                                 dst_ref.at[pl.ds(dst_row, 1), :], sem)


def _rows_wait(src_ref, dst_ref, n_rows, sem):
    pltpu.make_async_copy(src_ref.at[pl.ds(0, n_rows), :], dst_ref.at[pl.ds(0, n_rows), :],
                          sem).wait()


def _dispatch_kernel(dp_ref, ds_ref, pad_ref, hp_ref, hs_ref, xs_ref, zero_s, sem, *, n_tiles,
                     n_blocks):
    i = pl.program_id(0)
    zr = zero_s.shape[0]

    def scatter_tile(d_ref, src_ref):
        rows = src_ref.shape[0]

        def issue(t, carry):
            for k in range(TOP_K):
                _row_copy(src_ref, t, xs_ref, d_ref[0, 0, t * TOP_K + k], sem).start()
            return carry

        lax.fori_loop(0, rows, issue, 0)
        for _ in range(TOP_K):
            _rows_wait(src_ref, xs_ref, rows, sem)

    @pl.when(i < n_tiles)
    def _():
        scatter_tile(dp_ref, hp_ref)

    @pl.when(i == n_tiles)
    def _():
        scatter_tile(ds_ref, hs_ref)
        zero_s[...] = jnp.zeros_like(zero_s)

        def pad_expert(e, carry):
            lo, hi = pad_ref[0, e], pad_ref[1, e]

            def issue(r, c):
                _row_copy(zero_s, 0, xs_ref, r, sem).start()
                return c

            def drain(r, c):
                _row_copy(zero_s, 0, xs_ref, r, sem).wait()
                return c

            lax.fori_loop(lo, hi, issue, 0)
            lax.fori_loop(lo, hi, drain, 0)
            return carry

        lax.fori_loop(0, N_EXPERTS, pad_expert, 0)

        def pad_block(b, carry):
            copies = [pltpu.make_async_copy(
                zero_s, xs_ref.at[pl.ds(pl.multiple_of(b * MOE_BM + q * zr, zr), zr), :], sem)
                for q in range(MOE_BM // zr)]
            for cp in copies:
                cp.start()
            for cp in copies:
                cp.wait()
            return carry

        lax.fori_loop(pad_ref[2, 0], n_blocks, pad_block, 0)


def _dispatch(dest_p, dest_s, pad_tab, hn_p, hn_s, n_blocks, *, tm):
    n_p, d = hn_p.shape
    n_s = hn_s.shape[0]
    n_tiles = n_p // tm
    last = n_tiles - 1
    return pl.pallas_call(
        functools.partial(_dispatch_kernel, n_tiles=n_tiles, n_blocks=n_blocks),
        grid=(n_tiles + 1,),
        in_specs=[pl.BlockSpec((1, 1, tm * TOP_K), lambda i: (jnp.minimum(i, last), 0, 0),
                               memory_space=pltpu.SMEM),
                  pl.BlockSpec((1, 1, n_s * TOP_K), lambda i: (0, 0, 0), memory_space=pltpu.SMEM),
                  pl.BlockSpec(memory_space=pltpu.SMEM),
                  pl.BlockSpec((tm, d), lambda i: (jnp.minimum(i, last), 0)),
                  pl.BlockSpec((n_s, d), lambda i: (0, 0))],
        out_specs=pl.BlockSpec(memory_space=pl.ANY),
        out_shape=jax.ShapeDtypeStruct((n_blocks * MOE_BM, d), F32),
        scratch_shapes=[pltpu.VMEM((64, d), F32), pltpu.SemaphoreType.DMA(())],
        compiler_params=pltpu.CompilerParams(
            dimension_semantics=("arbitrary",), has_side_effects=True,
            vmem_limit_bytes=VMEM_LIMIT),
        name="dispatch",
    )(dest_p.reshape(n_tiles, 1, tm * TOP_K), dest_s.reshape(1, 1, n_s * TOP_K), pad_tab,
      hn_p, hn_s)


def _experts_kernel(be_ref, nu_ref, x_ref, wg_ref, wl_ref, bg_ref, bl_ref, wd_ref, bd_ref,
                    y_ref, xb_s):
    del be_ref
    b = pl.program_id(0)
    j = pl.program_id(1)

    @pl.when(b < nu_ref[0])
    def _():
        @pl.when(j == 0)
        def _():
            xb_s[...] = x_ref[...].astype(BF16)

        xb = xb_s[...]
        glu = jnp.dot(xb, wg_ref[...], preferred_element_type=F32) + bg_ref[...]
        lin = jnp.dot(xb, wl_ref[...], preferred_element_type=F32) + bl_ref[...]
        glu = jnp.minimum(glu, SWIGLU_LIMIT)
        lin = jnp.clip(lin, -SWIGLU_LIMIT, SWIGLU_LIMIT)
        act = glu * jax.nn.sigmoid(SWIGLU_ALPHA * glu) * (lin + 1.0)
        part = jnp.dot(act.astype(BF16), wd_ref[...], preferred_element_type=F32)

        @pl.when(j == 0)
        def _():
            y_ref[...] = part + bd_ref[...]

        @pl.when(j > 0)
        def _():
            y_ref[...] = y_ref[...] + part

    @pl.when(jnp.logical_and(b >= nu_ref[0], j == 0))
    def _():
        y_ref[...] = jnp.zeros_like(y_ref)


def _experts(block_e, n_used, xs, w_gate_up, b_gate_up, w_down, b_down):
    rows, d = xs.shape
    d_ff = w_down.shape[1]
    nb = rows // MOE_BM
    nj = d_ff // MOE_TF

    def blk(b, nu):
        return jnp.minimum(b, nu[0] - 1)

    def jj(b, j, nu):
        return jnp.where(b < nu[0], j, nj - 1)

    grid_spec = pltpu.PrefetchScalarGridSpec(
        num_scalar_prefetch=2,
        grid=(nb, nj),
        in_specs=[
            pl.BlockSpec((MOE_BM, d), lambda b, j, be, nu: (blk(b, nu), 0)),
            pl.BlockSpec((None, d, MOE_TF), lambda b, j, be, nu: (be[blk(b, nu)], 0, jj(b, j, nu))),
            pl.BlockSpec((None, d, MOE_TF),
                         lambda b, j, be, nu: (be[blk(b, nu)], 0, nj + jj(b, j, nu))),
            pl.BlockSpec((None, 1, MOE_TF), lambda b, j, be, nu: (be[blk(b, nu)], 0, jj(b, j, nu))),
            pl.BlockSpec((None, 1, MOE_TF),
                         lambda b, j, be, nu: (be[blk(b, nu)], 0, nj + jj(b, j, nu))),
            pl.BlockSpec((None, MOE_TF, d), lambda b, j, be, nu: (be[blk(b, nu)], jj(b, j, nu), 0)),
            pl.BlockSpec((None, 1, d), lambda b, j, be, nu: (be[blk(b, nu)], 0, 0)),
        ],
        out_specs=pl.BlockSpec((MOE_BM, d), lambda b, j, be, nu: (b, 0)),
        scratch_shapes=[pltpu.VMEM((MOE_BM, d), BF16)],
    )
    return pl.pallas_call(
        _experts_kernel,
        grid_spec=grid_spec,
        out_shape=jax.ShapeDtypeStruct((rows, d), F32),
        compiler_params=pltpu.CompilerParams(
            dimension_semantics=("arbitrary", "arbitrary"), vmem_limit_bytes=VMEM_LIMIT),
        name="experts",
    )(block_e, n_used, xs, w_gate_up, w_gate_up, b_gate_up, b_gate_up, w_down, b_down)


def _combine_kernel(dest_ref, dnext_ref, gate_ref, h_ref, pe_ref, gpost_ref, gpre_ref, gple_ref,
                    wgate_ref, wproj_ref, ys_ref, out_ref, buf0, buf1, sem, *, n_tiles):
    i = pl.program_id(0)
    tm = h_ref.shape[0]

    def fetch(d_ref, buf, s):
        for t in range(tm):
            for k in range(TOP_K):
                _row_copy(ys_ref, d_ref[0, 0, t * TOP_K + k], buf.at[k], t, sem.at[s]).start()

    def arrived(buf, s):
        for k in range(TOP_K):
            _rows_wait(ys_ref, buf.at[k], tm, sem.at[s])

    def compute(buf):
        gts = gate_ref[...]
        ffn = gts[:, 0:1] * buf[0]
        for k in range(1, TOP_K):
            ffn = ffn + gts[:, k:k + 1] * buf[k]
        h = h_ref[...] + _rms(ffn, gpost_ref[...])
        gate = jax.nn.sigmoid(_mm(_rms(h, gpre_ref[...]), wgate_ref[...]))
        pp = _mm(pe_ref[...], wproj_ref[...])
        out_ref[...] = h + _rms(gate * pp, gple_ref[...])

    @pl.when(i == 0)
    def _():
        def issue(t, carry):
            for k in range(TOP_K):
                _row_copy(ys_ref, dest_ref[0, 0, t * TOP_K + k], buf0.at[k], t, sem.at[0]).start()
            return carry

        lax.fori_loop(0, tm, issue, 0)

    def step(cur, cur_s, nxt, nxt_s):
        arrived(cur, cur_s)
        fetch(dnext_ref, nxt, nxt_s)
        compute(cur)

    @pl.when(i % 2 == 0)
    def _():
        step(buf0, 0, buf1, 1)

    @pl.when(i % 2 == 1)
    def _():
        step(buf1, 1, buf0, 0)

    @pl.when(i == n_tiles - 1)
    def _():
        if (n_tiles - 1) % 2 == 0:
            arrived(buf1, 1)
        else:
            arrived(buf0, 0)


def _combine(dest, gates, h, pe, g_post, g_pre, g_ple, w_gate, w_proj, ys, *, tm):
    rows, d = h.shape
    pdim = pe.shape[1]
    nt = rows // tm
    dest3 = dest.reshape(nt, 1, tm * TOP_K)
    return pl.pallas_call(
        functools.partial(_combine_kernel, n_tiles=nt),
        grid=(nt,),
        in_specs=[pl.BlockSpec((1, 1, tm * TOP_K), lambda i: (i, 0, 0), memory_space=pltpu.SMEM),
                  pl.BlockSpec((1, 1, tm * TOP_K), lambda i: (jnp.minimum(i + 1, nt - 1), 0, 0),
                               memory_space=pltpu.SMEM),
                  pl.BlockSpec((tm, LANES), lambda i: (i, 0)),
                  pl.BlockSpec((tm, d), lambda i: (i, 0)),
                  pl.BlockSpec((tm, pdim), lambda i: (i, 0)),
                  _resident((1, d)), _resident((1, d)), _resident((1, d)),
                  _resident(w_gate.shape), _resident(w_proj.shape),
                  pl.BlockSpec(memory_space=pl.ANY)],
        out_specs=pl.BlockSpec((tm, d), lambda i: (i, 0)),
        out_shape=jax.ShapeDtypeStruct((rows, d), F32),
        scratch_shapes=[pltpu.VMEM((TOP_K, tm, d), F32), pltpu.VMEM((TOP_K, tm, d), F32),
                        pltpu.SemaphoreType.DMA((2,))],
        compiler_params=pltpu.CompilerParams(
            dimension_semantics=("arbitrary",), vmem_limit_bytes=VMEM_LIMIT),
        name="combine",
    )(dest3, dest3, gates, h, pe, g_post, g_pre, g_ple, w_gate, w_proj, ys)


def _spatial_params(w_spatial, b_spatial, length, dtype):
    tril = jnp.tril(jnp.ones((length, length), dtype=bool))
    w = jnp.where(tril, w_spatial[:, :length, :length], 0.0).astype(dtype)
    b = jnp.broadcast_to(b_spatial[:, :length, None], (G_B, length, DG_B)).astype(F32)
    return w, b


def _token_stage(x, s0, lb, w, *, cdt, tm_proj, tm_mix, c_len, tt, pairwise, c0, emit_vn):
    bsz, t, d = x.shape
    x2d = x.reshape(bsz * t, d)
    length = min(t, CHUNK_B)
    wsp, bsp = _spatial_params(w["w_spatial"], w["b_spatial"], length, BF16)
    outs = _proj(x2d, w["g_mix_pre"], w["w_in"], w["g_gmlp_v"], wsp, bsp,
                 cdt=cdt, chunk=length, tm=tm_proj, emit_vn=emit_vn)
    p, f = outs[0], outs[1]
    vn = outs[2] if emit_vn else None
    ya, s_new = _hgrn(p, f, lb, w["g_hgrn_out"], s0, bsz=bsz, t=t, c_len=c_len, tt=tt,
                      pairwise=pairwise)
    h, hn, meta, gates, cnt = _mix(
        ya, p, x2d, w["w_branch_a"], w["w_branch_b"], w["w_out"], w["g_mix_post"],
        w["g_ffn_pre"], w["router_w"], w["router_b"], c0, tm=tm_mix)
    return h, hn, meta, gates, cnt, s_new, vn


def _layer(xp, xs, pe_p, pe_s, s0_s, lb, w):
    bp, tp, d = xp.shape
    bs, ts, _ = xs.shape
    n_p, n_s = bp * tp, bs * ts

    zero_cnt = jnp.zeros((1, N_EXPERTS), F32)
    s0_p = jnp.zeros((bp, H_A, DK, DV), F32)
    h_p, hn_p, meta_p, gates_p, cnt_p, st_p, _ = _token_stage(
        xp, s0_p, lb, w, cdt=BF16, tm_proj=512, tm_mix=256, c_len=32, tt=256, pairwise=False,
        c0=zero_cnt, emit_vn=False)
    h_s, hn_s, meta_s, gates_s, cnt, st_s, vn_s = _token_stage(
        xs, s0_s, lb, w, cdt=F32, tm_proj=n_s, tm_mix=n_s, c_len=ts, tt=ts, pairwise=True,
        c0=cnt_p, emit_vn=True)

    counts = cnt[0].astype(I32)
    padded = (counts + MOE_BM - 1) // MOE_BM * MOE_BM
    pend = jnp.cumsum(padded)
    offs = pend - padded
    n_pairs = (n_p + n_s) * TOP_K
    nb = (n_pairs + N_EXPERTS * (MOE_BM - 1) + MOE_BM - 1) // MOE_BM
    block_e = jnp.minimum(jnp.searchsorted(pend, jnp.arange(nb, dtype=I32) * MOE_BM, side="right"),
                          N_EXPERTS - 1).astype(I32)
    n_used = (pend[-1:] // MOE_BM).astype(I32)

    def dest_of(meta):
        eidx, rank = meta[:, :TOP_K], meta[:, TOP_K:2 * TOP_K]
        hit = eidx[:, :, None] == jnp.arange(N_EXPERTS, dtype=I32)[None, None, :]
        return jnp.sum(jnp.where(hit, offs[None, None, :], 0), axis=-1) + rank

    dest_p, dest_s = dest_of(meta_p), dest_of(meta_s)
    pad_tab = jnp.stack([offs + counts, pend, jnp.broadcast_to(n_used, (N_EXPERTS,))])

    xs_rows = _dispatch(dest_p, dest_s, pad_tab, hn_p, hn_s, nb, tm=512)
    ys_rows = _experts(block_e, n_used, xs_rows, w["w_gate_up"], w["b_gate_up"][:, None, :],
                       w["w_down"], w["b_down"][:, None, :])

    wg, wp = w["w_ple_gate"], w["w_ple_proj"]
    y_p = _combine(dest_p, gates_p, h_p, pe_p.reshape(n_p, -1), w["g_ffn_post"], w["g_ple_pre"],
                   w["g_ple_post"], wg, wp, ys_rows, tm=128)
    y_s = _combine(dest_s, gates_s, h_s, pe_s.reshape(n_s, -1), w["g_ffn_post"], w["g_ple_pre"],
                   w["g_ple_post"], wg, wp, ys_rows, tm=n_s)
    return (y_p.reshape(bp, tp, d), y_s.reshape(bs, ts, d), st_p, st_s,
            vn_s.reshape(bs, ts, D_B))


def kernel(x_prompt, x_sample, state_hgrn, p_prompt, p_sample, norm_mix_pre, norm_mix_post,
           norm_ffn_pre, norm_ffn_post, norm_ple_pre, norm_ple_post, w_in, lb_logits,
           hgrn_out_norm, gmlp_v_norm, w_spatial, b_spatial, w_branch_a, w_branch_b, w_out,
           router_w, router_b, w_gate_up, b_gate_up, w_down, b_down, w_ple_proj, w_ple_gate):
    depth = w_in.shape[0]
    lb_all = jnp.cumsum(jax.nn.softmax(lb_logits.astype(F32), axis=0), axis=0)
    y_p, y_s = x_prompt, x_sample
    st_p_rows, st_s_rows, vn_rows = [], [], []
    for l in range(depth):
        row = lambda a: a[l][None, :]
        mat = lambda a: a[l].astype(BF16)
        w = dict(g_mix_pre=row(norm_mix_pre), g_mix_post=row(norm_mix_post),
                 g_ffn_pre=row(norm_ffn_pre), g_ffn_post=row(norm_ffn_post),
                 g_ple_pre=row(norm_ple_pre), g_ple_post=row(norm_ple_post),
                 w_in=mat(w_in), g_hgrn_out=row(hgrn_out_norm), g_gmlp_v=row(gmlp_v_norm),
                 w_spatial=w_spatial[l], b_spatial=b_spatial[l], w_branch_a=mat(w_branch_a),
                 w_branch_b=mat(w_branch_b), w_out=mat(w_out), router_w=mat(router_w),
                 router_b=row(router_b), w_gate_up=mat(w_gate_up), b_gate_up=b_gate_up[l],
                 w_down=mat(w_down), b_down=b_down[l], w_ple_proj=mat(w_ple_proj),
                 w_ple_gate=mat(w_ple_gate))
        y_p, y_s, st_p, st_s, vn_s = _layer(y_p, y_s, p_prompt[l], p_sample[l],
                                            state_hgrn[l].astype(F32), row(lb_all), w)
        st_p_rows.append(st_p)
        st_s_rows.append(st_s)
        vn_rows.append(vn_s)
    return (y_p, y_s, jnp.stack(st_p_rows, axis=0), jnp.stack(st_s_rows, axis=0),
            jnp.stack(vn_rows, axis=0))
```

```python
import functools

import jax
import jax.numpy as jnp
from jax import lax
from jax.experimental import pallas as pl
from jax.experimental.pallas import tpu as pltpu

F32 = jnp.float32
BF16 = jnp.bfloat16
I32 = jnp.int32

EPS = 1e-6
H_A = 8
DK = 128
DV = 128
D_A = H_A * DV
G_B = 8
DG_B = 128
D_B = G_B * DG_B
CHUNK_B = 128
N_EXPERTS = 32
TOP_K = 4
SWIGLU_LIMIT = 7.0
SWIGLU_ALPHA = 1.702

LANES = 128
SEG = 1024
MOE_BM = 512
MOE_TF = 1024
VMEM_LIMIT = 56 * 1024 * 1024


def _mm(a, b):
    return jnp.dot(a.astype(BF16), b.astype(BF16), preferred_element_type=F32)


def _mm_nt(a, b):
    return lax.dot_general(a.astype(BF16), b.astype(BF16), (((1,), (1,)), ((), ())),
                           preferred_element_type=F32)


def _mm_tn(a, b):
    return lax.dot_general(a.astype(BF16), b.astype(BF16), (((0,), (0,)), ((), ())),
                           preferred_element_type=F32)


def _rms(x, g):
    return x * lax.rsqrt(jnp.mean(x * x, axis=-1, keepdims=True) + EPS) * g


def _resident(shape):
    zeros = (0,) * len(shape)
    return pl.BlockSpec(shape, lambda *_: zeros, pipeline_mode=pl.Buffered(1))


def _proj_col(j):
    return j - (j >= 1).astype(I32) - (j >= 4).astype(I32)


def _proj_kernel(x_ref, g_ref, w_ref, gv_ref, wsp_ref, bsp_ref, p_ref, f_ref, *rest,
                 chunk, emit_vn):
    if emit_vn:
        vn_ref, xn_s, zu_s, vn_s = rest
    else:
        xn_s, zu_s, vn_s = rest
    j = pl.program_id(1)
    tm = x_ref.shape[0]
    cdt = p_ref.dtype

    @pl.when(j == 0)
    def _():
        xn_s[...] = _rms(x_ref[...], g_ref[...]).astype(xn_s.dtype)

    def acc():
        return _mm(xn_s[...], w_ref[...])

    @pl.when(j == 0)
    def _():
        a = acc()
        p_ref[...] = (a * jax.nn.sigmoid(a)).astype(cdt)

    @pl.when(j == 1)
    def _():
        f_ref[...] = acc()

    @pl.when(j == 2)
    def _():
        p_ref[...] = acc().astype(cdt)

    @pl.when(j == 3)
    def _():
        p_ref[...] = jax.nn.sigmoid(acc()).astype(cdt)

    @pl.when(j == 4)
    def _():
        zu_s[...] = jax.nn.gelu(acc()).astype(zu_s.dtype)

    @pl.when(j == 5)
    def _():
        gv = jax.nn.gelu(acc())
        xc = gv - jnp.mean(gv, axis=-1, keepdims=True)
        vn = xc * lax.rsqrt(jnp.mean(xc * xc, axis=-1, keepdims=True) + EPS) * gv_ref[...]
        vn_s[...] = vn
        if emit_vn:
            vn_ref[...] = vn
        for c in range(tm // chunk):
            rows = slice(c * chunk, (c + 1) * chunk)
            for g in range(G_B):
                cols = slice(g * DG_B, (g + 1) * DG_B)
                m = _mm(wsp_ref[g], vn_s[rows, cols]) + bsp_ref[g]
                p_ref[rows, cols] = (zu_s[rows, cols].astype(F32) * m).astype(cdt)

    @pl.when(j >= 6)
    def _():
        p_ref[...] = jax.nn.sigmoid(acc()).astype(cdt)


def _proj(x2d, g_pre, w_in, g_v, wsp, bsp, *, cdt, chunk, tm, emit_vn):
    rows, d = x2d.shape
    n_in = w_in.shape[1]
    nj = n_in // SEG
    out_shape = [jax.ShapeDtypeStruct((rows, (nj - 2) * SEG), cdt),
                 jax.ShapeDtypeStruct((rows, SEG), F32)]
    out_specs = [pl.BlockSpec((tm, SEG), lambda i, j: (i, _proj_col(j))),
                 pl.BlockSpec((tm, SEG), lambda i, j: (i, 0))]
    if emit_vn:
        out_shape.append(jax.ShapeDtypeStruct((rows, SEG), F32))
        out_specs.append(pl.BlockSpec((tm, SEG), lambda i, j: (i, 0)))
    return pl.pallas_call(
        functools.partial(_proj_kernel, chunk=chunk, emit_vn=emit_vn),
        grid=(rows // tm, nj),
        in_specs=[pl.BlockSpec((tm, d), lambda i, j: (i, 0), pipeline_mode=pl.Buffered(1)),
                  pl.BlockSpec((1, d), lambda i, j: (0, 0)),
                  pl.BlockSpec((d, SEG), lambda i, j: (0, j)),
                  pl.BlockSpec((1, SEG), lambda i, j: (0, 0)),
                  pl.BlockSpec((G_B, chunk, chunk), lambda i, j: (0, 0, 0)),
                  pl.BlockSpec((G_B, chunk, DG_B), lambda i, j: (0, 0, 0))],
        out_specs=out_specs,
        out_shape=out_shape,
        scratch_shapes=[pltpu.VMEM((tm, d), BF16), pltpu.VMEM((tm, SEG), cdt),
                        pltpu.VMEM((tm, SEG), F32)],
        compiler_params=pltpu.CompilerParams(
            dimension_semantics=("arbitrary", "arbitrary"), vmem_limit_bytes=VMEM_LIMIT),
        name="proj",
    )(x2d, g_pre, w_in, g_v, wsp, bsp)


def _scores_pairwise(q, k, b):
    c_len = q.shape[0]
    qb = q.astype(BF16).astype(F32)
    t_i = lax.broadcasted_iota(I32, b.shape, 0)
    lane = lax.broadcasted_iota(I32, (c_len, c_len), 1)
    sc = jnp.zeros((c_len, c_len), F32)
    for s in range(c_len):
        diff = jnp.where(t_i >= s, b - b[s:s + 1, :], -jnp.inf)
        dk = (jnp.exp(diff) * k[s:s + 1, :]).astype(BF16).astype(F32)
        sc = jnp.where(lane == s, jnp.sum(qb * dk, axis=-1, keepdims=True), sc)
    return sc


def _hgrn_kernel(*refs, c_len, pairwise, n_ride):
    (q_ref, i_ref, og_ref, f_ref, lb_ref, g_ref, tri_ref, s0_ref), refs = refs[:8], refs[8:]
    ride_in, refs = refs[:n_ride], refs[n_ride:]
    (ya_ref, sout_ref), refs = refs[:2], refs[2:]
    ride_out, refs = refs[:n_ride], refs[n_ride:]
    st_s, qs_s, ks_s, qb_s, kl_s, dl_s, o_s = refs
    ti = pl.program_id(1)
    nt = pl.num_programs(1)

    for src, dst in zip(ride_in, ride_out):
        dst[...] = src[...].astype(BF16)

    @pl.when(ti == 0)
    def _():
        for h in range(H_A):
            st_s[h] = s0_ref[h].T

    tt = q_ref.shape[0]
    nc = tt // c_len
    mid = c_len // 2

    lb = lb_ref[...]
    fg = lb + (1.0 - lb) * jax.nn.sigmoid(f_ref[...])
    k = 1.0 - fg
    logf = jnp.log(fg)
    hi = logf.astype(BF16)
    rem = logf - hi.astype(F32)
    md = rem.astype(BF16)
    lo = (rem - md.astype(F32)).astype(BF16)
    tri = tri_ref[...]
    b = (jnp.dot(tri, hi, preferred_element_type=F32) + jnp.dot(tri, md, preferred_element_type=F32)
         + jnp.dot(tri, lo, preferred_element_type=F32))
    q = q_ref[...].astype(F32)
    b3 = b.reshape(nc, c_len, D_A)
    k3 = k.reshape(nc, c_len, D_A)
    b_last = b3[:, c_len - 1:c_len, :]
    qb_s[...] = (q * jnp.exp(b)).astype(BF16)
    kl_s[...] = (k3 * jnp.exp(b_last - b3)).reshape(tt, D_A).astype(BF16)
    dl_s[...] = jnp.exp(b_last)
    if not pairwise:
        b_mid = b3[:, mid - 1:mid, :]
        qs_s[...] = (q.reshape(nc, c_len, D_A) * jnp.exp(b3 - b_mid)).reshape(tt, D_A).astype(BF16)
        ks_s[...] = (k3 * jnp.exp(b_mid - b3)).reshape(tt, D_A).astype(BF16)
    r_i = lax.broadcasted_iota(I32, (c_len, c_len), 0)
    c_i = lax.broadcasted_iota(I32, (c_len, c_len), 1)
    causal = r_i >= c_i

    for c in range(nc):
        rows = slice(c * c_len, (c + 1) * c_len)
        dl = dl_s[c]
        for h in range(H_A):
            hl = slice(h * DK, (h + 1) * DK)
            if pairwise:
                sc = _scores_pairwise(q[:, hl], k[:, hl], b[:, hl])
            else:
                sc = jnp.where(causal, _mm_nt(qs_s[rows, hl], ks_s[rows, hl]), 0.0)
            st = st_s[h]
            v = i_ref[rows, hl]
            o_s[rows, hl] = _mm(sc, v) + _mm_nt(qb_s[rows, hl], st)
            st_s[h] = dl[:, hl] * st + _mm_tn(v, kl_s[rows, hl])

    for h in range(H_A):
        hl = slice(h * DK, (h + 1) * DK)
        on = _rms(o_s[:, hl], g_ref[...])
        ya_ref[:, hl] = (on * og_ref[:, hl].astype(F32)).astype(ya_ref.dtype)

    @pl.when(ti == nt - 1)
    def _():
        for h in range(H_A):
            sout_ref[h] = st_s[h].T


def _hgrn(p, f, lb, g_out, s0, riders=(), *, bsz, t, c_len, tt, pairwise):
    assert t % tt == 0 and tt % c_len == 0 and (not pairwise or tt == c_len)
    nt = t // tt
    nc = tt // c_len
    cdt = p.dtype
    r_i = lax.broadcasted_iota(I32, (tt, tt), 0)
    c_i = lax.broadcasted_iota(I32, (tt, tt), 1)
    tri = jnp.logical_and(r_i >= c_i, r_i // c_len == c_i // c_len).astype(BF16)
    row = lambda b, ti: b * nt + ti
    steps = bsz * nt
    assert all(r.shape[0] % (8 * steps) == 0 for r in riders)
    ride_specs = [pl.BlockSpec((r.shape[0] // steps, r.shape[1]), lambda b, ti: (row(b, ti), 0))
                  for r in riders]
    return pl.pallas_call(
        functools.partial(_hgrn_kernel, c_len=c_len, pairwise=pairwise, n_ride=len(riders)),
        grid=(bsz, nt),
        in_specs=[pl.BlockSpec((tt, D_A), lambda b, ti: (row(b, ti), 0)),
                  pl.BlockSpec((tt, D_A), lambda b, ti: (row(b, ti), 1)),
                  pl.BlockSpec((tt, D_A), lambda b, ti: (row(b, ti), 2)),
                  pl.BlockSpec((tt, D_A), lambda b, ti: (row(b, ti), 0)),
                  pl.BlockSpec((1, D_A), lambda b, ti: (0, 0)),
                  pl.BlockSpec((1, DV), lambda b, ti: (0, 0)),
                  pl.BlockSpec((tt, tt), lambda b, ti: (0, 0)),
                  pl.BlockSpec((None, H_A, DK, DV), lambda b, ti: (b, 0, 0, 0))] + ride_specs,
        out_specs=[pl.BlockSpec((tt, D_A), lambda b, ti: (row(b, ti), 0)),
                   pl.BlockSpec((None, H_A, DK, DV), lambda b, ti: (b, 0, 0, 0))] + ride_specs,
        out_shape=[jax.ShapeDtypeStruct((bsz * t, D_A), cdt),
                   jax.ShapeDtypeStruct((bsz, H_A, DK, DV), F32)]
        + [jax.ShapeDtypeStruct(r.shape, BF16) for r in riders],
        scratch_shapes=[pltpu.VMEM((H_A, DV, DK), F32),
                        pltpu.VMEM((tt, D_A), BF16), pltpu.VMEM((tt, D_A), BF16),
                        pltpu.VMEM((tt, D_A), BF16), pltpu.VMEM((tt, D_A), BF16),
                        pltpu.VMEM((nc, 1, D_A), F32), pltpu.VMEM((tt, D_A), F32)],
        compiler_params=pltpu.CompilerParams(
            dimension_semantics=("arbitrary", "arbitrary"), vmem_limit_bytes=VMEM_LIMIT),
        name="hgrn",
    )(p, p, p, f, lb, g_out, tri, s0, *riders)


def _mix_kernel(ya_ref, yb_ref, sga_ref, sgb_ref, x_ref, wa_ref, wb_ref, wo_ref, gpost_ref,
                gpre_ref, rw_ref, rb_ref, c0_ref, h_ref, hn_ref, meta_ref, gate_ref, cnt_ref,
                carry_s):
    i = pl.program_id(0)
    tm = x_ref.shape[0]

    @pl.when(i == 0)
    def _():
        carry_s[...] = c0_ref[...]

    ma = _mm(ya_ref[...], wa_ref[...])
    mb = _mm(yb_ref[...], wb_ref[...])
    mixed = sga_ref[...].astype(F32) * ma + sgb_ref[...].astype(F32) * mb
    h = x_ref[...] + _rms(_mm(mixed, wo_ref[...]), gpost_ref[...])
    h_ref[...] = h
    hn = _rms(h, gpre_ref[...])
    hn_ref[...] = hn
    logits = _mm(hn, rw_ref[...]) + rb_ref[...]

    e_i = lax.broadcasted_iota(I32, (tm, N_EXPERTS), 1)
    work = logits
    vals, idxs = [], []
    onehot = jnp.zeros((tm, N_EXPERTS), F32)
    for _ in range(TOP_K):
        mx = jnp.max(work, axis=-1, keepdims=True)
        idx = jnp.min(jnp.where(work == mx, e_i, N_EXPERTS), axis=-1, keepdims=True)
        sel = e_i == idx
        vals.append(mx)
        idxs.append(idx)
        onehot = onehot + sel.astype(F32)
        work = jnp.where(sel, -jnp.inf, work)
    exps = [jnp.exp(v - vals[0]) for v in vals]
    denom = exps[0] + exps[1] + exps[2] + exps[3]

    r_i = lax.broadcasted_iota(I32, (tm, tm), 0)
    c_i = lax.broadcasted_iota(I32, (tm, tm), 1)
    before = (r_i > c_i).astype(BF16)
    prefix = jnp.dot(before, onehot.astype(BF16), preferred_element_type=F32) + carry_s[...]
    carry_s[...] = carry_s[...] + jnp.sum(onehot, axis=0, keepdims=True)
    cnt_ref[...] = carry_s[...]

    lane = lax.broadcasted_iota(I32, (tm, LANES), 1)
    meta = jnp.zeros((tm, LANES), I32)
    gts = jnp.zeros((tm, LANES), F32)
    for k in range(TOP_K):
        rank = jnp.sum(jnp.where(e_i == idxs[k], prefix, 0.0), axis=-1, keepdims=True)
        meta = jnp.where(lane == k, idxs[k], meta)
        meta = jnp.where(lane == TOP_K + k, rank.astype(I32), meta)
        gts = jnp.where(lane == k, exps[k] / denom, gts)
    meta_ref[...] = meta
    gate_ref[...] = gts


def _mix(ya, p, x2d, wa, wb, wo, g_post, g_pre, rw, rb, c0, *, tm):
    rows, d = x2d.shape
    wide = d // SEG
    return pl.pallas_call(
        _mix_kernel,
        grid=(rows // tm,),
        in_specs=[pl.BlockSpec((tm, D_A), lambda i: (i, 0)),
                  pl.BlockSpec((tm, D_B), lambda i: (i, 3)),
                  pl.BlockSpec((tm, d), lambda i: (i, 4 // wide)),
                  pl.BlockSpec((tm, d), lambda i: (i, 4 // wide + 1)),
                  pl.BlockSpec((tm, d), lambda i: (i, 0)),
                  _resident(wa.shape), _resident(wb.shape), _resident(wo.shape),
                  _resident((1, d)), _resident((1, d)),
                  _resident(rw.shape), _resident((1, N_EXPERTS)), _resident((1, N_EXPERTS))],
        out_specs=[pl.BlockSpec((tm, d), lambda i: (i, 0)),
                   pl.BlockSpec((tm, d), lambda i: (i, 0)),
                   pl.BlockSpec((tm, LANES), lambda i: (i, 0)),
                   pl.BlockSpec((tm, LANES), lambda i: (i, 0)),
                   pl.BlockSpec((1, N_EXPERTS), lambda i: (0, 0))],
        out_shape=[jax.ShapeDtypeStruct((rows, d), F32),
                   jax.ShapeDtypeStruct((rows, d), F32),
                   jax.ShapeDtypeStruct((rows, LANES), I32),
                   jax.ShapeDtypeStruct((rows, LANES), F32),
                   jax.ShapeDtypeStruct((1, N_EXPERTS), F32)],
        scratch_shapes=[pltpu.VMEM((1, N_EXPERTS), F32)],
        compiler_params=pltpu.CompilerParams(
            dimension_semantics=("arbitrary",), vmem_limit_bytes=VMEM_LIMIT),
        name="mix",
    )(ya, p, p, p, x2d, wa, wb, wo, g_post, g_pre, rw, rb, c0)


def _row_copy(src_ref, src_row, dst_ref, dst_row, sem):
    return pltpu.make_async_copy(src_ref.at[pl.ds(src_row, 1), :],
                                 dst_ref.at[pl.ds(dst_row, 1), :], sem)


def _rows_wait(src_ref, dst_ref, n_rows, sem):
    pltpu.make_async_copy(src_ref.at[pl.ds(0, n_rows), :], dst_ref.at[pl.ds(0, n_rows), :],
                          sem).wait()


def _dispatch_kernel(dp_ref, ds_ref, pad_ref, hp_ref, hs_ref, xs_ref, zero_s, sem, *, n_tiles,
                     n_blocks):
    i = pl.program_id(0)
    zr = zero_s.shape[0]

    def scatter_tile(d_ref, src_ref):
        rows = src_ref.shape[0]

        def issue(t, carry):
            for k in range(TOP_K):
                _row_copy(src_ref, t, xs_ref, d_ref[0, 0, t * TOP_K + k], sem).start()
            return carry

        lax.fori_loop(0, rows, issue, 0)
        for _ in range(TOP_K):
            _rows_wait(src_ref, xs_ref, rows, sem)

    @pl.when(i < n_tiles)
    def _():
        scatter_tile(dp_ref, hp_ref)

    @pl.when(i == n_tiles)
    def _():
        scatter_tile(ds_ref, hs_ref)
        zero_s[...] = jnp.zeros_like(zero_s)

        def pad_expert(e, carry):
            lo, hi = pad_ref[0, e], pad_ref[1, e]

            def issue(r, c):
                _row_copy(zero_s, 0, xs_ref, r, sem).start()
                return c

            def drain(r, c):
                _row_copy(zero_s, 0, xs_ref, r, sem).wait()
                return c

            lax.fori_loop(lo, hi, issue, 0)
            lax.fori_loop(lo, hi, drain, 0)
            return carry

        lax.fori_loop(0, N_EXPERTS, pad_expert, 0)

        def pad_block(b, carry):
            copies = [pltpu.make_async_copy(
                zero_s, xs_ref.at[pl.ds(pl.multiple_of(b * MOE_BM + q * zr, zr), zr), :], sem)
                for q in range(MOE_BM // zr)]
            for cp in copies:
                cp.start()
            for cp in copies:
                cp.wait()
            return carry

        lax.fori_loop(pad_ref[2, 0], n_blocks, pad_block, 0)


def _dispatch(dest_p, dest_s, pad_tab, hn_p, hn_s, n_blocks, *, tm):
    n_p, d = hn_p.shape
    n_s = hn_s.shape[0]
    n_tiles = n_p // tm
    last = n_tiles - 1
    return pl.pallas_call(
        functools.partial(_dispatch_kernel, n_tiles=n_tiles, n_blocks=n_blocks),
        grid=(n_tiles + 1,),
        in_specs=[pl.BlockSpec((1, 1, tm * TOP_K), lambda i: (jnp.minimum(i, last), 0, 0),
                               memory_space=pltpu.SMEM),
                  pl.BlockSpec((1, 1, n_s * TOP_K), lambda i: (0, 0, 0), memory_space=pltpu.SMEM),
                  pl.BlockSpec(memory_space=pltpu.SMEM),
                  pl.BlockSpec((tm, d), lambda i: (jnp.minimum(i, last), 0)),
                  pl.BlockSpec((n_s, d), lambda i: (0, 0))],
        out_specs=pl.BlockSpec(memory_space=pl.ANY),
        out_shape=jax.ShapeDtypeStruct((n_blocks * MOE_BM, d), F32),
        scratch_shapes=[pltpu.VMEM((64, d), F32), pltpu.SemaphoreType.DMA(())],
        compiler_params=pltpu.CompilerParams(
            dimension_semantics=("arbitrary",), has_side_effects=True,
            vmem_limit_bytes=VMEM_LIMIT),
        name="dispatch",
    )(dest_p.reshape(n_tiles, 1, tm * TOP_K), dest_s.reshape(1, 1, n_s * TOP_K), pad_tab,
      hn_p, hn_s)


def _experts_kernel(be_ref, nu_ref, x_ref, wg_ref, wl_ref, bg_ref, bl_ref, wd_ref, bd_ref,
                    y_ref, xb_s):
    del be_ref
    b = pl.program_id(0)
    j = pl.program_id(1)

    @pl.when(b < nu_ref[0])
    def _():
        @pl.when(j == 0)
        def _():
            xb_s[...] = x_ref[...].astype(BF16)

        xb = xb_s[...]
        glu = jnp.dot(xb, wg_ref[...], preferred_element_type=F32) + bg_ref[...]
        lin = jnp.dot(xb, wl_ref[...], preferred_element_type=F32) + bl_ref[...]
        glu = jnp.minimum(glu, SWIGLU_LIMIT)
        lin = jnp.clip(lin, -SWIGLU_LIMIT, SWIGLU_LIMIT)
        act = glu * jax.nn.sigmoid(SWIGLU_ALPHA * glu) * (lin + 1.0)
        part = jnp.dot(act.astype(BF16), wd_ref[...], preferred_element_type=F32)

        @pl.when(j == 0)
        def _():
            y_ref[...] = part + bd_ref[...]

        @pl.when(j > 0)
        def _():
            y_ref[...] = y_ref[...] + part

    @pl.when(jnp.logical_and(b >= nu_ref[0], j == 0))
    def _():
        y_ref[...] = jnp.zeros_like(y_ref)


def _experts(block_e, n_used, xs, w_gate_up, b_gate_up, w_down, b_down):
    rows, d = xs.shape
    d_ff = w_down.shape[1]
    nb = rows // MOE_BM
    nj = d_ff // MOE_TF

    def blk(b, nu):
        return jnp.minimum(b, nu[0] - 1)

    def jj(b, j, nu):
        return jnp.where(b < nu[0], j, nj - 1)

    grid_spec = pltpu.PrefetchScalarGridSpec(
        num_scalar_prefetch=2,
        grid=(nb, nj),
        in_specs=[
            pl.BlockSpec((MOE_BM, d), lambda b, j, be, nu: (blk(b, nu), 0)),
            pl.BlockSpec((None, d, MOE_TF), lambda b, j, be, nu: (be[blk(b, nu)], 0, jj(b, j, nu))),
            pl.BlockSpec((None, d, MOE_TF),
                         lambda b, j, be, nu: (be[blk(b, nu)], 0, nj + jj(b, j, nu))),
            pl.BlockSpec((None, 1, MOE_TF), lambda b, j, be, nu: (be[blk(b, nu)], 0, jj(b, j, nu))),
            pl.BlockSpec((None, 1, MOE_TF),
                         lambda b, j, be, nu: (be[blk(b, nu)], 0, nj + jj(b, j, nu))),
            pl.BlockSpec((None, MOE_TF, d), lambda b, j, be, nu: (be[blk(b, nu)], jj(b, j, nu), 0)),
            pl.BlockSpec((None, 1, d), lambda b, j, be, nu: (be[blk(b, nu)], 0, 0)),
        ],
        out_specs=pl.BlockSpec((MOE_BM, d), lambda b, j, be, nu: (b, 0)),
        scratch_shapes=[pltpu.VMEM((MOE_BM, d), BF16)],
    )
    return pl.pallas_call(
        _experts_kernel,
        grid_spec=grid_spec,
        out_shape=jax.ShapeDtypeStruct((rows, d), F32),
        compiler_params=pltpu.CompilerParams(
            dimension_semantics=("arbitrary", "arbitrary"), vmem_limit_bytes=VMEM_LIMIT),
        name="experts",
    )(block_e, n_used, xs, w_gate_up, w_gate_up, b_gate_up, b_gate_up, w_down, b_down)


def _combine_kernel(dest_ref, dnext_ref, gate_ref, h_ref, pe_ref, gpost_ref, gpre_ref, gple_ref,
                    wgate_ref, wproj_ref, ys_ref, out_ref, buf0, buf1, sem, *, n_tiles):
    i = pl.program_id(0)
    tm = h_ref.shape[0]

    def fetch(d_ref, buf, s):
        for t in range(tm):
            for k in range(TOP_K):
                _row_copy(ys_ref, d_ref[0, 0, t * TOP_K + k], buf.at[k], t, sem.at[s]).start()

    def arrived(buf, s):
        for k in range(TOP_K):
            _rows_wait(ys_ref, buf.at[k], tm, sem.at[s])

    def compute(buf):
        gts = gate_ref[...]
        ffn = gts[:, 0:1] * buf[0]
        for k in range(1, TOP_K):
            ffn = ffn + gts[:, k:k + 1] * buf[k]
        h = h_ref[...] + _rms(ffn, gpost_ref[...])
        gate = jax.nn.sigmoid(_mm(_rms(h, gpre_ref[...]), wgate_ref[...]))
        pp = _mm(pe_ref[...], wproj_ref[...])
        out_ref[...] = h + _rms(gate * pp, gple_ref[...])

    @pl.when(i == 0)
    def _():
        def issue(t, carry):
            for k in range(TOP_K):
                _row_copy(ys_ref, dest_ref[0, 0, t * TOP_K + k], buf0.at[k], t, sem.at[0]).start()
            return carry

        lax.fori_loop(0, tm, issue, 0)

    def step(cur, cur_s, nxt, nxt_s):
        arrived(cur, cur_s)
        fetch(dnext_ref, nxt, nxt_s)
        compute(cur)

    @pl.when(i % 2 == 0)
    def _():
        step(buf0, 0, buf1, 1)

    @pl.when(i % 2 == 1)
    def _():
        step(buf1, 1, buf0, 0)

    @pl.when(i == n_tiles - 1)
    def _():
        if (n_tiles - 1) % 2 == 0:
            arrived(buf1, 1)
        else:
            arrived(buf0, 0)


def _combine(dest, gates, h, pe, g_post, g_pre, g_ple, w_gate, w_proj, ys, *, tm):
    rows, d = h.shape
    pdim = pe.shape[1]
    nt = rows // tm
    dest3 = dest.reshape(nt, 1, tm * TOP_K)
    return pl.pallas_call(
        functools.partial(_combine_kernel, n_tiles=nt),
        grid=(nt,),
        in_specs=[pl.BlockSpec((1, 1, tm * TOP_K), lambda i: (i, 0, 0), memory_space=pltpu.SMEM),
                  pl.BlockSpec((1, 1, tm * TOP_K), lambda i: (jnp.minimum(i + 1, nt - 1), 0, 0),
                               memory_space=pltpu.SMEM),
                  pl.BlockSpec((tm, LANES), lambda i: (i, 0)),
                  pl.BlockSpec((tm, d), lambda i: (i, 0)),
                  pl.BlockSpec((tm, pdim), lambda i: (i, 0)),
                  _resident((1, d)), _resident((1, d)), _resident((1, d)),
                  _resident(w_gate.shape), _resident(w_proj.shape),
                  pl.BlockSpec(memory_space=pl.ANY)],
        out_specs=pl.BlockSpec((tm, d), lambda i: (i, 0)),
        out_shape=jax.ShapeDtypeStruct((rows, d), F32),
        scratch_shapes=[pltpu.VMEM((TOP_K, tm, d), F32), pltpu.VMEM((TOP_K, tm, d), F32),
                        pltpu.SemaphoreType.DMA((2,))],
        compiler_params=pltpu.CompilerParams(
            dimension_semantics=("arbitrary",), vmem_limit_bytes=VMEM_LIMIT),
        name="combine",
    )(dest3, dest3, gates, h, pe, g_post, g_pre, g_ple, w_gate, w_proj, ys)


def _spatial_params(w_spatial, b_spatial, length, dtype):
    tril = jnp.tril(jnp.ones((length, length), dtype=bool))
    w = jnp.where(tril, w_spatial[:, :length, :length], 0.0).astype(dtype)
    b = jnp.broadcast_to(b_spatial[:, :length, None], (G_B, length, DG_B)).astype(F32)
    return w, b


def _token_stage(x, s0, lb, w, riders=(), *, cdt, tm_proj, tm_mix, c_len, tt, pairwise, c0,
                 emit_vn):
    bsz, t, d = x.shape
    x2d = x.reshape(bsz * t, d)
    length = min(t, CHUNK_B)
    wsp, bsp = _spatial_params(w["w_spatial"], w["b_spatial"], length, BF16)
    outs = _proj(x2d, w["g_mix_pre"], w["w_in"], w["g_gmlp_v"], wsp, bsp,
                 cdt=cdt, chunk=length, tm=tm_proj, emit_vn=emit_vn)
    p, f = outs[0], outs[1]
    vn = outs[2] if emit_vn else None
    ya, s_new, *rode = _hgrn(p, f, lb, w["g_hgrn_out"], s0, riders, bsz=bsz, t=t, c_len=c_len,
                             tt=tt, pairwise=pairwise)
    h, hn, meta, gates, cnt = _mix(
        ya, p, x2d, w["w_branch_a"], w["w_branch_b"], w["w_out"], w["g_mix_post"],
        w["g_ffn_pre"], w["router_w"], w["router_b"], c0, tm=tm_mix)
    return h, hn, meta, gates, cnt, s_new, vn, rode


def _layer(xp, xs, pe_p, pe_s, s0_s, lb, w):
    bp, tp, d = xp.shape
    bs, ts, _ = xs.shape
    n_p, n_s = bp * tp, bs * ts

    zero_cnt = jnp.zeros((1, N_EXPERTS), F32)
    s0_p = jnp.zeros((bp, H_A, DK, DV), F32)
    n_e, d_ff = w["w_down"].shape[:2]
    riders = (w["w_gate_up"].reshape(n_e * d, 2 * d_ff), w["w_down"].reshape(n_e * d_ff, d))
    h_p, hn_p, meta_p, gates_p, cnt_p, st_p, _, (w_gu, w_dn) = _token_stage(
        xp, s0_p, lb, w, riders, cdt=BF16, tm_proj=1024, tm_mix=256, c_len=32, tt=256, pairwise=False,
        c0=zero_cnt, emit_vn=False)
    h_s, hn_s, meta_s, gates_s, cnt, st_s, vn_s, _ = _token_stage(
        xs, s0_s, lb, w, cdt=F32, tm_proj=n_s, tm_mix=n_s, c_len=ts, tt=ts, pairwise=True,
        c0=cnt_p, emit_vn=True)

    counts = cnt[0].astype(I32)
    padded = (counts + MOE_BM - 1) // MOE_BM * MOE_BM
    pend = jnp.cumsum(padded)
    offs = pend - padded
    n_pairs = (n_p + n_s) * TOP_K
    nb = (n_pairs + N_EXPERTS * (MOE_BM - 1) + MOE_BM - 1) // MOE_BM
    first_row = jnp.arange(nb, dtype=I32)[:, None] * MOE_BM
    block_e = jnp.minimum(jnp.sum((first_row >= pend[None, :]).astype(I32), axis=1),
                          N_EXPERTS - 1)
    n_used = (pend[-1:] // MOE_BM).astype(I32)

    def dest_of(meta):
        eidx, rank = meta[:, :TOP_K].T, meta[:, TOP_K:2 * TOP_K].T
        base = jnp.zeros_like(eidx)
        for e in range(N_EXPERTS):
            base = jnp.where(eidx == e, offs[e], base)
        return (base + rank).T

    dest_p, dest_s = dest_of(meta_p), dest_of(meta_s)
    pad_tab = jnp.stack([offs + counts, pend, jnp.broadcast_to(n_used, (N_EXPERTS,))])

    xs_rows = _dispatch(dest_p, dest_s, pad_tab, hn_p, hn_s, nb, tm=512)
    ys_rows = _experts(block_e, n_used, xs_rows, w_gu.reshape(n_e, d, 2 * d_ff),
                       w["b_gate_up"][:, None, :], w_dn.reshape(n_e, d_ff, d),
                       w["b_down"][:, None, :])

    wg, wp = w["w_ple_gate"], w["w_ple_proj"]
    y_p = _combine(dest_p, gates_p, h_p, pe_p.reshape(n_p, -1), w["g_ffn_post"], w["g_ple_pre"],
                   w["g_ple_post"], wg, wp, ys_rows, tm=128)
    y_s = _combine(dest_s, gates_s, h_s, pe_s.reshape(n_s, -1), w["g_ffn_post"], w["g_ple_pre"],
                   w["g_ple_post"], wg, wp, ys_rows, tm=n_s)
    return (y_p.reshape(bp, tp, d), y_s.reshape(bs, ts, d), st_p, st_s,
            vn_s.reshape(bs, ts, D_B))


def kernel(x_prompt, x_sample, state_hgrn, p_prompt, p_sample, norm_mix_pre, norm_mix_post,
           norm_ffn_pre, norm_ffn_post, norm_ple_pre, norm_ple_post, w_in, lb_logits,
           hgrn_out_norm, gmlp_v_norm, w_spatial, b_spatial, w_branch_a, w_branch_b, w_out,
           router_w, router_b, w_gate_up, b_gate_up, w_down, b_down, w_ple_proj, w_ple_gate):
    depth = w_in.shape[0]
    lb_all = jnp.cumsum(jax.nn.softmax(lb_logits.astype(F32), axis=0), axis=0)
    y_p, y_s = x_prompt, x_sample
    st_p_rows, st_s_rows, vn_rows = [], [], []
    for l in range(depth):
        row = lambda a: a[l][None, :]
        mat = lambda a: a[l].astype(BF16)
        w = dict(g_mix_pre=row(norm_mix_pre), g_mix_post=row(norm_mix_post),
                 g_ffn_pre=row(norm_ffn_pre), g_ffn_post=row(norm_ffn_post),
                 g_ple_pre=row(norm_ple_pre), g_ple_post=row(norm_ple_post),
                 w_in=mat(w_in), g_hgrn_out=row(hgrn_out_norm), g_gmlp_v=row(gmlp_v_norm),
                 w_spatial=w_spatial[l], b_spatial=b_spatial[l], w_branch_a=mat(w_branch_a),
                 w_branch_b=mat(w_branch_b), w_out=mat(w_out), router_w=mat(router_w),
                 router_b=row(router_b), w_gate_up=w_gate_up[l], b_gate_up=b_gate_up[l],
                 w_down=w_down[l], b_down=b_down[l], w_ple_proj=mat(w_ple_proj),
                 w_ple_gate=mat(w_ple_gate))
        y_p, y_s, st_p, st_s, vn_s = _layer(y_p, y_s, p_prompt[l], p_sample[l],
                                            state_hgrn[l].astype(F32), row(lb_all), w)
        st_p_rows.append(st_p)
        st_s_rows.append(st_s)
        vn_rows.append(vn_s)
    return (y_p, y_s, jnp.stack(st_p_rows, axis=0), jnp.stack(st_s_rows, axis=0),
            jnp.stack(vn_rows, axis=0))
```

```python
import functools

import jax
import jax.numpy as jnp
from jax import lax
from jax.experimental import pallas as pl
from jax.experimental.pallas import tpu as pltpu

F32 = jnp.float32
BF16 = jnp.bfloat16
I32 = jnp.int32

EPS = 1e-6
H_A = 8
DK = 128
DV = 128
D_A = H_A * DV
G_B = 8
DG_B = 128
D_B = G_B * DG_B
CHUNK_B = 128
N_EXPERTS = 32
TOP_K = 4
SWIGLU_LIMIT = 7.0
SWIGLU_ALPHA = 1.702

LANES = 128
SEG = 1024
MOE_BM = 512
MOE_TF = 1024
VMEM_LIMIT = 56 * 1024 * 1024


def _mm(a, b):
    return jnp.dot(a.astype(BF16), b.astype(BF16), preferred_element_type=F32)


def _mm_nt(a, b):
    return lax.dot_general(a.astype(BF16), b.astype(BF16), (((1,), (1,)), ((), ())),
                           preferred_element_type=F32)


def _mm_tn(a, b):
    return lax.dot_general(a.astype(BF16), b.astype(BF16), (((0,), (0,)), ((), ())),
                           preferred_element_type=F32)


def _rms(x, g):
    return x * lax.rsqrt(jnp.mean(x * x, axis=-1, keepdims=True) + EPS) * g


def _resident(shape):
    zeros = (0,) * len(shape)
    return pl.BlockSpec(shape, lambda *_: zeros, pipeline_mode=pl.Buffered(1))


def _proj_col(j):
    return j - (j >= 1).astype(I32) - (j >= 4).astype(I32)


def _proj_kernel(x_ref, g_ref, w_ref, gv_ref, wsp_ref, bsp_ref, *rest, chunk, emit_vn, ride):
    ride_in, rest = (rest[0], rest[1:]) if ride else (None, rest)
    (p_ref, f_ref), rest = rest[:2], rest[2:]
    vn_ref, rest = (rest[0], rest[1:]) if emit_vn else (None, rest)
    ride_out, rest = (rest[0], rest[1:]) if ride else (None, rest)
    xn_s, zu_s, vn_s = rest
    j = pl.program_id(1)
    tm = x_ref.shape[0]
    cdt = p_ref.dtype

    if ride:
        ride_out[...] = ride_in[...].astype(BF16)

    @pl.when(j == 0)
    def _():
        xn_s[...] = _rms(x_ref[...], g_ref[...]).astype(xn_s.dtype)

    def acc():
        return _mm(xn_s[...], w_ref[...])

    @pl.when(j == 0)
    def _():
        a = acc()
        p_ref[...] = (a * jax.nn.sigmoid(a)).astype(cdt)

    @pl.when(j == 1)
    def _():
        f_ref[...] = acc()

    @pl.when(j == 2)
    def _():
        p_ref[...] = acc().astype(cdt)

    @pl.when(j == 3)
    def _():
        p_ref[...] = jax.nn.sigmoid(acc()).astype(cdt)

    @pl.when(j == 4)
    def _():
        zu_s[...] = jax.nn.gelu(acc()).astype(zu_s.dtype)

    @pl.when(j == 5)
    def _():
        gv = jax.nn.gelu(acc())
        xc = gv - jnp.mean(gv, axis=-1, keepdims=True)
        vn = xc * lax.rsqrt(jnp.mean(xc * xc, axis=-1, keepdims=True) + EPS) * gv_ref[...]
        vn_s[...] = vn
        if emit_vn:
            vn_ref[...] = vn
        for c in range(tm // chunk):
            rows = slice(c * chunk, (c + 1) * chunk)
            for g in range(G_B):
                cols = slice(g * DG_B, (g + 1) * DG_B)
                m = _mm(wsp_ref[g], vn_s[rows, cols]) + bsp_ref[g]
                p_ref[rows, cols] = (zu_s[rows, cols].astype(F32) * m).astype(cdt)

    @pl.when(j >= 6)
    def _():
        p_ref[...] = jax.nn.sigmoid(acc()).astype(cdt)


def _proj(x2d, g_pre, w_in, g_v, wsp, bsp, rider=None, *, cdt, chunk, tm, emit_vn):
    rows, d = x2d.shape
    n_in = w_in.shape[1]
    nj = n_in // SEG
    ride = rider is not None
    in_ride, args_ride = [], []
    out_shape = [jax.ShapeDtypeStruct((rows, (nj - 2) * SEG), cdt),
                 jax.ShapeDtypeStruct((rows, SEG), F32)]
    out_specs = [pl.BlockSpec((tm, SEG), lambda i, j: (i, _proj_col(j))),
                 pl.BlockSpec((tm, SEG), lambda i, j: (i, 0))]
    if emit_vn:
        out_shape.append(jax.ShapeDtypeStruct((rows, SEG), F32))
        out_specs.append(pl.BlockSpec((tm, SEG), lambda i, j: (i, 0)))
    if ride:
        slabs = (rows // tm) * (nj - 2)
        assert rider.shape[0] % (8 * slabs) == 0
        spec = pl.BlockSpec((rider.shape[0] // slabs, rider.shape[1]),
                            lambda i, j: (i * (nj - 2) + jnp.minimum(j, nj - 3), 0))
        in_ride, args_ride = [spec], [rider]
        out_specs.append(spec)
        out_shape.append(jax.ShapeDtypeStruct(rider.shape, BF16))
    return pl.pallas_call(
        functools.partial(_proj_kernel, chunk=chunk, emit_vn=emit_vn, ride=ride),
        grid=(rows // tm, nj),
        in_specs=[pl.BlockSpec((tm, d), lambda i, j: (i, 0), pipeline_mode=pl.Buffered(1)),
                  pl.BlockSpec((1, d), lambda i, j: (0, 0)),
                  pl.BlockSpec((d, SEG), lambda i, j: (0, j)),
                  pl.BlockSpec((1, SEG), lambda i, j: (0, 0)),
                  pl.BlockSpec((G_B, chunk, chunk), lambda i, j: (0, 0, 0)),
                  pl.BlockSpec((G_B, chunk, DG_B), lambda i, j: (0, 0, 0))] + in_ride,
        out_specs=out_specs,
        out_shape=out_shape,
        scratch_shapes=[pltpu.VMEM((tm, d), BF16), pltpu.VMEM((tm, SEG), cdt),
                        pltpu.VMEM((tm, SEG), F32)],
        compiler_params=pltpu.CompilerParams(
            dimension_semantics=("arbitrary", "arbitrary"), vmem_limit_bytes=VMEM_LIMIT),
        name="proj",
    )(x2d, g_pre, w_in, g_v, wsp, bsp, *args_ride)


def _scores_pairwise(q, k, b):
    c_len = q.shape[0]
    qb = q.astype(BF16).astype(F32)
    t_i = lax.broadcasted_iota(I32, b.shape, 0)
    lane = lax.broadcasted_iota(I32, (c_len, c_len), 1)
    sc = jnp.zeros((c_len, c_len), F32)
    for s in range(c_len):
        diff = jnp.where(t_i >= s, b - b[s:s + 1, :], -jnp.inf)
        dk = (jnp.exp(diff) * k[s:s + 1, :]).astype(BF16).astype(F32)
        sc = jnp.where(lane == s, jnp.sum(qb * dk, axis=-1, keepdims=True), sc)
    return sc


def _hgrn_kernel(*refs, c_len, pairwise, n_ride):
    (q_ref, i_ref, og_ref, f_ref, lb_ref, g_ref, tri_ref, s0_ref), refs = refs[:8], refs[8:]
    ride_in, refs = refs[:n_ride], refs[n_ride:]
    (ya_ref, sout_ref), refs = refs[:2], refs[2:]
    ride_out, refs = refs[:n_ride], refs[n_ride:]
    st_s, qs_s, ks_s, qb_s, kl_s, dl_s, o_s = refs
    ti = pl.program_id(1)
    nt = pl.num_programs(1)

    for src, dst in zip(ride_in, ride_out):
        dst[...] = src[...].astype(BF16)

    n_seq, tt = q_ref.shape[0], q_ref.shape[1]
    nc = tt // c_len
    mid = c_len // 2

    @pl.when(ti == 0)
    def _():
        for u in range(n_seq):
            for h in range(H_A):
                st_s[u * H_A + h] = s0_ref[u, h].T

    r_i = lax.broadcasted_iota(I32, (c_len, c_len), 0)
    c_i = lax.broadcasted_iota(I32, (c_len, c_len), 1)
    causal = r_i >= c_i
    lb = lb_ref[...]
    tri = tri_ref[...]

    prep = []
    for u in range(n_seq):
        fg = lb + (1.0 - lb) * jax.nn.sigmoid(f_ref[u])
        k = 1.0 - fg
        logf = jnp.log(fg)
        hi = logf.astype(BF16)
        rem = logf - hi.astype(F32)
        md = rem.astype(BF16)
        lo = (rem - md.astype(F32)).astype(BF16)
        b = (jnp.dot(tri, hi, preferred_element_type=F32)
             + jnp.dot(tri, md, preferred_element_type=F32)
             + jnp.dot(tri, lo, preferred_element_type=F32))
        q = q_ref[u].astype(F32)
        b3 = b.reshape(nc, c_len, D_A)
        k3 = k.reshape(nc, c_len, D_A)
        b_last = b3[:, c_len - 1:c_len, :]
        qb_s[u] = (q * jnp.exp(b)).astype(BF16)
        kl_s[u] = (k3 * jnp.exp(b_last - b3)).reshape(tt, D_A).astype(BF16)
        dl_s[u * nc:(u + 1) * nc] = jnp.exp(b_last)
        if pairwise:
            prep.append((q, k, b))
        else:
            b_mid = b3[:, mid - 1:mid, :]
            qs_s[u] = (q.reshape(nc, c_len, D_A) * jnp.exp(b3 - b_mid)).reshape(tt, D_A).astype(BF16)
            ks_s[u] = (k3 * jnp.exp(b_mid - b3)).reshape(tt, D_A).astype(BF16)

    for c in range(nc):
        rows = slice(c * c_len, (c + 1) * c_len)
        for u in range(n_seq):
            dl = dl_s[u * nc + c]
            for h in range(H_A):
                hl = slice(h * DK, (h + 1) * DK)
                if pairwise:
                    q, k, b = prep[u]
                    sc = _scores_pairwise(q[:, hl], k[:, hl], b[:, hl])
                else:
                    sc = jnp.where(causal, _mm_nt(qs_s[u, rows, hl], ks_s[u, rows, hl]), 0.0)
                st = st_s[u * H_A + h]
                v = i_ref[u, rows, hl]
                o_s[u, rows, hl] = _mm(sc, v) + _mm_nt(qb_s[u, rows, hl], st)
                st_s[u * H_A + h] = dl[:, hl] * st + _mm_tn(v, kl_s[u, rows, hl])

    for u in range(n_seq):
        for h in range(H_A):
            hl = slice(h * DK, (h + 1) * DK)
            on = _rms(o_s[u, :, hl], g_ref[...])
            ya_ref[u, :, hl] = (on * og_ref[u, :, hl].astype(F32)).astype(ya_ref.dtype)

    @pl.when(ti == nt - 1)
    def _():
        for u in range(n_seq):
            for h in range(H_A):
                sout_ref[u, h] = st_s[u * H_A + h].T


def _hgrn(p, f, lb, g_out, s0, riders=(), *, bsz, t, c_len, tt, pairwise):
    n_seq = 2 if bsz % 2 == 0 else 1
    assert t % tt == 0 and tt % c_len == 0 and (not pairwise or tt == c_len)
    nt = t // tt
    nc = tt // c_len
    cdt = p.dtype
    r_i = lax.broadcasted_iota(I32, (tt, tt), 0)
    c_i = lax.broadcasted_iota(I32, (tt, tt), 1)
    tri = jnp.logical_and(r_i >= c_i, r_i // c_len == c_i // c_len).astype(BF16)
    steps = (bsz // n_seq) * nt
    assert all(r.shape[0] % (8 * steps) == 0 for r in riders)
    ride_specs = [pl.BlockSpec((r.shape[0] // steps, r.shape[1]), lambda b, ti: (b * nt + ti, 0))
                  for r in riders]
    p3 = p.reshape(bsz, t, p.shape[1])
    seq = lambda col: pl.BlockSpec((n_seq, tt, D_A), lambda b, ti: (b, ti, col))
    state = pl.BlockSpec((n_seq, H_A, DK, DV), lambda b, ti: (b, 0, 0, 0))
    ya, s_new, *rode = pl.pallas_call(
        functools.partial(_hgrn_kernel, c_len=c_len, pairwise=pairwise, n_ride=len(riders)),
        grid=(bsz // n_seq, nt),
        in_specs=[seq(0), seq(1), seq(2),
                  seq(0),
                  pl.BlockSpec((1, D_A), lambda b, ti: (0, 0)),
                  pl.BlockSpec((1, DV), lambda b, ti: (0, 0)),
                  pl.BlockSpec((tt, tt), lambda b, ti: (0, 0)),
                  state] + ride_specs,
        out_specs=[seq(0), state] + ride_specs,
        out_shape=[jax.ShapeDtypeStruct((bsz, t, D_A), cdt),
                   jax.ShapeDtypeStruct((bsz, H_A, DK, DV), F32)]
        + [jax.ShapeDtypeStruct(r.shape, BF16) for r in riders],
        scratch_shapes=[pltpu.VMEM((n_seq * H_A, DV, DK), F32),
                        pltpu.VMEM((n_seq, tt, D_A), BF16), pltpu.VMEM((n_seq, tt, D_A), BF16),
                        pltpu.VMEM((n_seq, tt, D_A), BF16), pltpu.VMEM((n_seq, tt, D_A), BF16),
                        pltpu.VMEM((n_seq * nc, 1, D_A), F32),
                        pltpu.VMEM((n_seq, tt, D_A), F32)],
        compiler_params=pltpu.CompilerParams(
            dimension_semantics=("arbitrary", "arbitrary"), vmem_limit_bytes=VMEM_LIMIT),
        name="hgrn",
    )(p3, p3, p3, f.reshape(bsz, t, D_A), lb, g_out, tri, s0, *riders)
    return (ya.reshape(bsz * t, D_A), s_new, *rode)


def _merge_kernel(ya_ref, yb_ref, sga_ref, sgb_ref, wa_ref, wb_ref, m_ref):
    ma = _mm(ya_ref[...], wa_ref[...])
    mb = _mm(yb_ref[...], wb_ref[...])
    mixed = sga_ref[...].astype(F32) * ma + sgb_ref[...].astype(F32) * mb
    m_ref[...] = mixed.astype(m_ref.dtype)


def _merge(ya, p, wa, wb, *, tm):
    rows = ya.shape[0]
    d = wa.shape[1]
    wide = d // SEG
    return pl.pallas_call(
        _merge_kernel,
        grid=(rows // tm,),
        in_specs=[pl.BlockSpec((tm, D_A), lambda i: (i, 0)),
                  pl.BlockSpec((tm, D_B), lambda i: (i, 3)),
                  pl.BlockSpec((tm, d), lambda i: (i, 4 // wide)),
                  pl.BlockSpec((tm, d), lambda i: (i, 4 // wide + 1)),
                  _resident(wa.shape), _resident(wb.shape)],
        out_specs=pl.BlockSpec((tm, d), lambda i: (i, 0)),
        out_shape=jax.ShapeDtypeStruct((rows, d), p.dtype),
        compiler_params=pltpu.CompilerParams(
            dimension_semantics=("arbitrary",), vmem_limit_bytes=VMEM_LIMIT),
        name="merge",
    )(ya, p, p, p, wa, wb)


def _mix_kernel(m_ref, x_ref, wo_ref, gpost_ref, gpre_ref, rw_ref, rb_ref, c0_ref,
                h_ref, hn_ref, meta_ref, gate_ref, cnt_ref, carry_s):
    i = pl.program_id(0)
    tm = x_ref.shape[0]

    @pl.when(i == 0)
    def _():
        carry_s[...] = c0_ref[...]

    h = x_ref[...] + _rms(_mm(m_ref[...], wo_ref[...]), gpost_ref[...])
    h_ref[...] = h
    hn = _rms(h, gpre_ref[...])
    hn_ref[...] = hn
    logits = _mm(hn, rw_ref[...]) + rb_ref[...]

    e_i = lax.broadcasted_iota(I32, (tm, N_EXPERTS), 1)
    work = logits
    vals, idxs = [], []
    onehot = jnp.zeros((tm, N_EXPERTS), F32)
    for _ in range(TOP_K):
        mx = jnp.max(work, axis=-1, keepdims=True)
        idx = jnp.min(jnp.where(work == mx, e_i, N_EXPERTS), axis=-1, keepdims=True)
        sel = e_i == idx
        vals.append(mx)
        idxs.append(idx)
        onehot = onehot + sel.astype(F32)
        work = jnp.where(sel, -jnp.inf, work)
    exps = [jnp.exp(v - vals[0]) for v in vals]
    denom = exps[0] + exps[1] + exps[2] + exps[3]

    r_i = lax.broadcasted_iota(I32, (tm, tm), 0)
    c_i = lax.broadcasted_iota(I32, (tm, tm), 1)
    before = (r_i > c_i).astype(BF16)
    prefix = jnp.dot(before, onehot.astype(BF16), preferred_element_type=F32) + carry_s[...]
    carry_s[...] = carry_s[...] + jnp.sum(onehot, axis=0, keepdims=True)
    cnt_ref[...] = carry_s[...]

    lane = lax.broadcasted_iota(I32, (tm, LANES), 1)
    meta = jnp.zeros((tm, LANES), I32)
    gts = jnp.zeros((tm, LANES), F32)
    for k in range(TOP_K):
        rank = jnp.sum(jnp.where(e_i == idxs[k], prefix, 0.0), axis=-1, keepdims=True)
        meta = jnp.where(lane == k, idxs[k], meta)
        meta = jnp.where(lane == TOP_K + k, rank.astype(I32), meta)
        gts = jnp.where(lane == k, exps[k] / denom, gts)
    meta_ref[...] = meta
    gate_ref[...] = gts


def _mix(mixed, x2d, wo, g_post, g_pre, rw, rb, c0, *, tm):
    rows, d = x2d.shape
    return pl.pallas_call(
        _mix_kernel,
        grid=(rows // tm,),
        in_specs=[pl.BlockSpec((tm, d), lambda i: (i, 0)),
                  pl.BlockSpec((tm, d), lambda i: (i, 0)),
                  _resident(wo.shape),
                  _resident((1, d)), _resident((1, d)),
                  _resident(rw.shape), _resident((1, N_EXPERTS)), _resident((1, N_EXPERTS))],
        out_specs=[pl.BlockSpec((tm, d), lambda i: (i, 0)),
                   pl.BlockSpec((tm, d), lambda i: (i, 0)),
                   pl.BlockSpec((tm, LANES), lambda i: (i, 0)),
                   pl.BlockSpec((tm, LANES), lambda i: (i, 0)),
                   pl.BlockSpec((1, N_EXPERTS), lambda i: (0, 0))],
        out_shape=[jax.ShapeDtypeStruct((rows, d), F32),
                   jax.ShapeDtypeStruct((rows, d), F32),
                   jax.ShapeDtypeStruct((rows, LANES), I32),
                   jax.ShapeDtypeStruct((rows, LANES), F32),
                   jax.ShapeDtypeStruct((1, N_EXPERTS), F32)],
        scratch_shapes=[pltpu.VMEM((1, N_EXPERTS), F32)],
        compiler_params=pltpu.CompilerParams(
            dimension_semantics=("arbitrary",), vmem_limit_bytes=VMEM_LIMIT),
        name="mix",
    )(mixed, x2d, wo, g_post, g_pre, rw, rb, c0)


def _row_copy(src_ref, src_row, dst_ref, dst_row, sem):
    return pltpu.make_async_copy(src_ref.at[pl.ds(src_row, 1), :],
                                 dst_ref.at[pl.ds(dst_row, 1), :], sem)


def _rows_wait(src_ref, dst_ref, n_rows, sem):
    pltpu.make_async_copy(src_ref.at[pl.ds(0, n_rows), :], dst_ref.at[pl.ds(0, n_rows), :],
                          sem).wait()


def _dispatch_kernel(dp_ref, ds_ref, pad_ref, hp_ref, hs_ref, xs_ref, zero_s, sem, *, n_tiles,
                     n_blocks):
    i = pl.program_id(0)
    zr = zero_s.shape[0]

    def scatter_tile(d_ref, src_ref):
        rows = src_ref.shape[0]

        def issue(t, carry):
            for k in range(TOP_K):
                _row_copy(src_ref, t, xs_ref, d_ref[0, 0, t * TOP_K + k], sem).start(priority=k % 2)
            return carry

        lax.fori_loop(0, rows, issue, 0)
        for _ in range(TOP_K):
            _rows_wait(src_ref, xs_ref, rows, sem)

    @pl.when(i < n_tiles)
    def _():
        scatter_tile(dp_ref, hp_ref)

    @pl.when(i == n_tiles)
    def _():
        scatter_tile(ds_ref, hs_ref)
        zero_s[...] = jnp.zeros_like(zero_s)

        def pad_expert(e, carry):
            lo, hi = pad_ref[0, e], pad_ref[1, e]

            def issue(r, c):
                _row_copy(zero_s, 0, xs_ref, r, sem).start()
                return c

            def drain(r, c):
                _row_copy(zero_s, 0, xs_ref, r, sem).wait()
                return c

            lax.fori_loop(lo, hi, issue, 0)
            lax.fori_loop(lo, hi, drain, 0)
            return carry

        lax.fori_loop(0, N_EXPERTS, pad_expert, 0)

        def pad_block(b, carry):
            copies = [pltpu.make_async_copy(
                zero_s, xs_ref.at[pl.ds(pl.multiple_of(b * MOE_BM + q * zr, zr), zr), :], sem)
                for q in range(MOE_BM // zr)]
            for cp in copies:
                cp.start()
            for cp in copies:
                cp.wait()
            return carry

        lax.fori_loop(pad_ref[2, 0], n_blocks, pad_block, 0)


def _dispatch(dest_p, dest_s, pad_tab, hn_p, hn_s, n_blocks, *, tm):
    n_p, d = hn_p.shape
    n_s = hn_s.shape[0]
    n_tiles = n_p // tm
    last = n_tiles - 1
    return pl.pallas_call(
        functools.partial(_dispatch_kernel, n_tiles=n_tiles, n_blocks=n_blocks),
        grid=(n_tiles + 1,),
        in_specs=[pl.BlockSpec((1, 1, tm * TOP_K), lambda i: (jnp.minimum(i, last), 0, 0),
                               memory_space=pltpu.SMEM),
                  pl.BlockSpec((1, 1, n_s * TOP_K), lambda i: (0, 0, 0), memory_space=pltpu.SMEM),
                  pl.BlockSpec(memory_space=pltpu.SMEM),
                  pl.BlockSpec((tm, d), lambda i: (jnp.minimum(i, last), 0)),
                  pl.BlockSpec((n_s, d), lambda i: (0, 0))],
        out_specs=pl.BlockSpec(memory_space=pl.ANY),
        out_shape=jax.ShapeDtypeStruct((n_blocks * MOE_BM, d), F32),
        scratch_shapes=[pltpu.VMEM((64, d), F32), pltpu.SemaphoreType.DMA(())],
        compiler_params=pltpu.CompilerParams(
            dimension_semantics=("arbitrary",), has_side_effects=True,
            vmem_limit_bytes=VMEM_LIMIT),
        name="dispatch",
    )(dest_p.reshape(n_tiles, 1, tm * TOP_K), dest_s.reshape(1, 1, n_s * TOP_K), pad_tab,
      hn_p, hn_s)


def _experts_kernel(be_ref, nu_ref, x_ref, wg_ref, wl_ref, bg_ref, bl_ref, wd_ref, bd_ref,
                    y_ref):
    del be_ref
    b = pl.program_id(0)
    d_ff = wd_ref.shape[0]

    @pl.when(b < nu_ref[0])
    def _():
        xb = x_ref[...].astype(BF16)
        y = None
        for c in range(d_ff // MOE_TF):
            cols = slice(c * MOE_TF, (c + 1) * MOE_TF)
            glu = jnp.dot(xb, wg_ref[:, cols], preferred_element_type=F32) + bg_ref[:, cols]
            lin = jnp.dot(xb, wl_ref[:, cols], preferred_element_type=F32) + bl_ref[:, cols]
            glu = jnp.minimum(glu, SWIGLU_LIMIT)
            lin = jnp.clip(lin, -SWIGLU_LIMIT, SWIGLU_LIMIT)
            act = glu * jax.nn.sigmoid(SWIGLU_ALPHA * glu) * (lin + 1.0)
            part = jnp.dot(act.astype(BF16), wd_ref[cols, :], preferred_element_type=F32)
            y = part + bd_ref[...] if y is None else y + part
        y_ref[...] = y

    @pl.when(b >= nu_ref[0])
    def _():
        y_ref[...] = jnp.zeros_like(y_ref)


def _experts(block_e, n_used, xs, w_gate_up, b_gate_up, w_down, b_down):
    rows, d = xs.shape
    d_ff = w_down.shape[1]
    nb = rows // MOE_BM

    def blk(b, nu):
        return jnp.minimum(b, nu[0] - 1)

    def held(shape, index_map):
        return pl.BlockSpec(shape, index_map, pipeline_mode=pl.Buffered(1))

    grid_spec = pltpu.PrefetchScalarGridSpec(
        num_scalar_prefetch=2,
        grid=(nb,),
        in_specs=[
            pl.BlockSpec((MOE_BM, d), lambda b, be, nu: (blk(b, nu), 0)),
            held((None, d, d_ff), lambda b, be, nu: (be[blk(b, nu)], 0, 0)),
            held((None, d, d_ff), lambda b, be, nu: (be[blk(b, nu)], 0, 1)),
            held((None, 1, d_ff), lambda b, be, nu: (be[blk(b, nu)], 0, 0)),
            held((None, 1, d_ff), lambda b, be, nu: (be[blk(b, nu)], 0, 1)),
            held((None, d_ff, d), lambda b, be, nu: (be[blk(b, nu)], 0, 0)),
            held((None, 1, d), lambda b, be, nu: (be[blk(b, nu)], 0, 0)),
        ],
        out_specs=pl.BlockSpec((MOE_BM, d), lambda b, be, nu: (b, 0)),
    )
    return pl.pallas_call(
        _experts_kernel,
        grid_spec=grid_spec,
        out_shape=jax.ShapeDtypeStruct((rows, d), F32),
        compiler_params=pltpu.CompilerParams(
            dimension_semantics=("arbitrary",), vmem_limit_bytes=VMEM_LIMIT),
        name="experts",
    )(block_e, n_used, xs, w_gate_up, w_gate_up, b_gate_up, b_gate_up, w_down, b_down)


def _combine_kernel(dest_ref, dnext_ref, gate_ref, h_ref, pe_ref, gpost_ref, gpre_ref, gple_ref,
                    wgate_ref, wproj_ref, ys_ref, out_ref, buf0, buf1, sem, *, n_tiles):
    i = pl.program_id(0)
    tm = h_ref.shape[0]

    def fetch(d_ref, buf, s):
        for t in range(tm):
            for k in range(TOP_K):
                _row_copy(ys_ref, d_ref[0, 0, t * TOP_K + k], buf.at[k], t,
                          sem.at[s]).start(priority=k % 2)

    def arrived(buf, s):
        for k in range(TOP_K):
            _rows_wait(ys_ref, buf.at[k], tm, sem.at[s])

    def compute(buf):
        gts = gate_ref[...]
        ffn = gts[:, 0:1] * buf[0]
        for k in range(1, TOP_K):
            ffn = ffn + gts[:, k:k + 1] * buf[k]
        h = h_ref[...] + _rms(ffn, gpost_ref[...])
        gate = jax.nn.sigmoid(_mm(_rms(h, gpre_ref[...]), wgate_ref[...]))
        pp = _mm(pe_ref[...], wproj_ref[...])
        out_ref[...] = h + _rms(gate * pp, gple_ref[...])

    @pl.when(i == 0)
    def _():
        def issue(t, carry):
            for k in range(TOP_K):
                _row_copy(ys_ref, dest_ref[0, 0, t * TOP_K + k], buf0.at[k], t,
                          sem.at[0]).start(priority=k % 2)
            return carry

        lax.fori_loop(0, tm, issue, 0)

    def step(cur, cur_s, nxt, nxt_s):
        arrived(cur, cur_s)
        fetch(dnext_ref, nxt, nxt_s)
        compute(cur)

    @pl.when(i % 2 == 0)
    def _():
        step(buf0, 0, buf1, 1)

    @pl.when(i % 2 == 1)
    def _():
        step(buf1, 1, buf0, 0)

    @pl.when(i == n_tiles - 1)
    def _():
        if (n_tiles - 1) % 2 == 0:
            arrived(buf1, 1)
        else:
            arrived(buf0, 0)


def _combine(dest, gates, h, pe, g_post, g_pre, g_ple, w_gate, w_proj, ys, *, tm):
    rows, d = h.shape
    pdim = pe.shape[1]
    nt = rows // tm
    dest3 = dest.reshape(nt, 1, tm * TOP_K)
    return pl.pallas_call(
        functools.partial(_combine_kernel, n_tiles=nt),
        grid=(nt,),
        in_specs=[pl.BlockSpec((1, 1, tm * TOP_K), lambda i: (i, 0, 0), memory_space=pltpu.SMEM),
                  pl.BlockSpec((1, 1, tm * TOP_K), lambda i: (jnp.minimum(i + 1, nt - 1), 0, 0),
                               memory_space=pltpu.SMEM),
                  pl.BlockSpec((tm, LANES), lambda i: (i, 0)),
                  pl.BlockSpec((tm, d), lambda i: (i, 0)),
                  pl.BlockSpec((tm, pdim), lambda i: (i, 0)),
                  _resident((1, d)), _resident((1, d)), _resident((1, d)),
                  _resident(w_gate.shape), _resident(w_proj.shape),
                  pl.BlockSpec(memory_space=pl.ANY)],
        out_specs=pl.BlockSpec((tm, d), lambda i: (i, 0)),
        out_shape=jax.ShapeDtypeStruct((rows, d), F32),
        scratch_shapes=[pltpu.VMEM((TOP_K, tm, d), F32), pltpu.VMEM((TOP_K, tm, d), F32),
                        pltpu.SemaphoreType.DMA((2,))],
        compiler_params=pltpu.CompilerParams(
            dimension_semantics=("arbitrary",), vmem_limit_bytes=VMEM_LIMIT),
        name="combine",
    )(dest3, dest3, gates, h, pe, g_post, g_pre, g_ple, w_gate, w_proj, ys)


def _spatial_params(w_spatial, b_spatial, length, dtype):
    tril = jnp.tril(jnp.ones((length, length), dtype=bool))
    w = jnp.where(tril, w_spatial[:, :length, :length], 0.0).astype(dtype)
    b = jnp.broadcast_to(b_spatial[:, :length, None], (G_B, length, DG_B)).astype(F32)
    return w, b


def _token_stage(x, s0, lb, w, riders=(None, ()), *, cdt, tm_proj, tm_mix, c_len, tt, pairwise,
                 c0, emit_vn):
    bsz, t, d = x.shape
    x2d = x.reshape(bsz * t, d)
    length = min(t, CHUNK_B)
    wsp, bsp = _spatial_params(w["w_spatial"], w["b_spatial"], length, BF16)
    outs = _proj(x2d, w["g_mix_pre"], w["w_in"], w["g_gmlp_v"], wsp, bsp, riders[0],
                 cdt=cdt, chunk=length, tm=tm_proj, emit_vn=emit_vn)
    p, f = outs[0], outs[1]
    vn = outs[2] if emit_vn else None
    ya, s_new, *rode = _hgrn(p, f, lb, w["g_hgrn_out"], s0, riders[1], bsz=bsz, t=t,
                             c_len=c_len, tt=tt, pairwise=pairwise)
    if riders[0] is not None:
        rode = [outs[-1]] + rode
    mixed = _merge(ya, p, w["w_branch_a"], w["w_branch_b"], tm=tm_mix)
    h, hn, meta, gates, cnt = _mix(mixed, x2d, w["w_out"], w["g_mix_post"], w["g_ffn_pre"],
                                   w["router_w"], w["router_b"], c0, tm=tm_mix)
    return h, hn, meta, gates, cnt, s_new, vn, rode


def _layer(xp, xs, pe_p, pe_s, s0_s, lb, w):
    bp, tp, d = xp.shape
    bs, ts, _ = xs.shape
    n_p, n_s = bp * tp, bs * ts

    zero_cnt = jnp.zeros((1, N_EXPERTS), F32)
    s0_p = jnp.zeros((bp, H_A, DK, DV), F32)
    n_e, d_ff = w["w_down"].shape[:2]
    riders = (w["w_down"].reshape(n_e * d_ff, d), (w["w_gate_up"].reshape(n_e * d, 2 * d_ff),))
    h_p, hn_p, meta_p, gates_p, cnt_p, st_p, _, (w_dn, w_gu) = _token_stage(
        xp, s0_p, lb, w, riders, cdt=BF16, tm_proj=1024, tm_mix=512, c_len=32, tt=128, pairwise=False,
        c0=zero_cnt, emit_vn=False)
    h_s, hn_s, meta_s, gates_s, cnt, st_s, vn_s, _ = _token_stage(
        xs, s0_s, lb, w, cdt=F32, tm_proj=n_s, tm_mix=n_s, c_len=ts, tt=ts, pairwise=True,
        c0=cnt_p, emit_vn=True)

    counts = cnt[0].astype(I32)
    padded = (counts + MOE_BM - 1) // MOE_BM * MOE_BM
    pend = jnp.cumsum(padded)
    offs = pend - padded
    n_pairs = (n_p + n_s) * TOP_K
    nb = (n_pairs + N_EXPERTS * (MOE_BM - 1) + MOE_BM - 1) // MOE_BM
    first_row = jnp.arange(nb, dtype=I32)[:, None] * MOE_BM
    block_e = jnp.minimum(jnp.sum((first_row >= pend[None, :]).astype(I32), axis=1),
                          N_EXPERTS - 1)
    n_used = (pend[-1:] // MOE_BM).astype(I32)

    def dest_of(meta):
        eidx, rank = meta[:, :TOP_K].T, meta[:, TOP_K:2 * TOP_K].T
        base = jnp.zeros_like(eidx)
        for e in range(N_EXPERTS):
            base = jnp.where(eidx == e, offs[e], base)
        return (base + rank).T

    dest_p, dest_s = dest_of(meta_p), dest_of(meta_s)
    pad_tab = jnp.stack([offs + counts, pend, jnp.broadcast_to(n_used, (N_EXPERTS,))])

    xs_rows = _dispatch(dest_p, dest_s, pad_tab, hn_p, hn_s, nb, tm=512)
    ys_rows = _experts(block_e, n_used, xs_rows, w_gu.reshape(n_e, d, 2 * d_ff),
                       w["b_gate_up"][:, None, :], w_dn.reshape(n_e, d_ff, d),
                       w["b_down"][:, None, :])

    wg, wp = w["w_ple_gate"], w["w_ple_proj"]
    y_p = _combine(dest_p, gates_p, h_p, pe_p.reshape(n_p, -1), w["g_ffn_post"], w["g_ple_pre"],
                   w["g_ple_post"], wg, wp, ys_rows, tm=256)
    y_s = _combine(dest_s, gates_s, h_s, pe_s.reshape(n_s, -1), w["g_ffn_post"], w["g_ple_pre"],
                   w["g_ple_post"], wg, wp, ys_rows, tm=n_s)
    return (y_p.reshape(bp, tp, d), y_s.reshape(bs, ts, d), st_p, st_s,
            vn_s.reshape(bs, ts, D_B))


def kernel(x_prompt, x_sample, state_hgrn, p_prompt, p_sample, norm_mix_pre, norm_mix_post,
           norm_ffn_pre, norm_ffn_post, norm_ple_pre, norm_ple_post, w_in, lb_logits,
           hgrn_out_norm, gmlp_v_norm, w_spatial, b_spatial, w_branch_a, w_branch_b, w_out,
           router_w, router_b, w_gate_up, b_gate_up, w_down, b_down, w_ple_proj, w_ple_gate):
    depth = w_in.shape[0]
    lb_all = jnp.cumsum(jax.nn.softmax(lb_logits.astype(F32), axis=0), axis=0)
    y_p, y_s = x_prompt, x_sample
    st_p_rows, st_s_rows, vn_rows = [], [], []
    for l in range(depth):
        row = lambda a: a[l][None, :]
        mat = lambda a: a[l].astype(BF16)
        w = dict(g_mix_pre=row(norm_mix_pre), g_mix_post=row(norm_mix_post),
                 g_ffn_pre=row(norm_ffn_pre), g_ffn_post=row(norm_ffn_post),
                 g_ple_pre=row(norm_ple_pre), g_ple_post=row(norm_ple_post),
                 w_in=mat(w_in), g_hgrn_out=row(hgrn_out_norm), g_gmlp_v=row(gmlp_v_norm),
                 w_spatial=w_spatial[l], b_spatial=b_spatial[l], w_branch_a=mat(w_branch_a),
                 w_branch_b=mat(w_branch_b), w_out=mat(w_out), router_w=mat(router_w),
                 router_b=row(router_b), w_gate_up=w_gate_up[l], b_gate_up=b_gate_up[l],
                 w_down=w_down[l], b_down=b_down[l], w_ple_proj=mat(w_ple_proj),
                 w_ple_gate=mat(w_ple_gate))
        y_p, y_s, st_p, st_s, vn_s = _layer(y_p, y_s, p_prompt[l], p_sample[l],
                                            state_hgrn[l].astype(F32), row(lb_all), w)
        st_p_rows.append(st_p)
        st_s_rows.append(st_s)
        vn_rows.append(vn_s)
    return (y_p, y_s, jnp.stack(st_p_rows, axis=0), jnp.stack(st_s_rows, axis=0),
            jnp.stack(vn_rows, axis=0))
```
